```python
import jax, jax.numpy as jnp
from jax import lax
import numpy as np

D_MODEL = 1024
BATCH = 8
SEQ = 4096
DEPTH = 2

N_A = DEPTH // 2
N_B = DEPTH - N_A
PLE_DIM = 256
FOX_HEADS = 16
FOX_HEAD_DIM = D_MODEL // FOX_HEADS
FOX_IN = 3 * D_MODEL + FOX_HEADS
FOX_GATE_BIAS = 3.0
MLA_HEADS = 16
QK_NOPE_DIM = 128
QK_ROPE_DIM = 64
V_HEAD_DIM = 128
Q_LORA_RANK = 384
KV_LORA_RANK = 256
ROPE_THETA = 10000.0
D_FF = 2816
BLOCK_Q = 128
LN_EPS = 1e-5
RMS_EPS = 1e-6
ALPHA = (2 * DEPTH) ** 0.25
BETA = (8 * DEPTH) ** -0.25

kernel_name = 'yoco_fox_mla_macaron_deepnorm'


def layer_norm(x, g, b):
    xf = x.astype(jnp.float32)
    mu = jnp.mean(xf, axis=-1, keepdims=True)
    xc = xf - mu
    var = jnp.mean(xc * xc, axis=-1, keepdims=True)
    y = xc * lax.rsqrt(var + LN_EPS) * g.astype(jnp.float32) + b.astype(jnp.float32)
    return y.astype(x.dtype)


def rms_norm(x, g):
    xf = x.astype(jnp.float32)
    y = xf * lax.rsqrt(jnp.mean(xf * xf, axis=-1, keepdims=True) + RMS_EPS) * g.astype(jnp.float32)
    return y.astype(x.dtype)


def post_norm(x, delta, g, b):
    return layer_norm(ALPHA * x + delta, g, b)


def swiglu(x, w_in, w_out):
    h = x @ w_in
    gate, up = h[..., :D_FF], h[..., D_FF:]
    return (jax.nn.silu(gate) * up) @ w_out


def rope_tables(seq_len):
    half = QK_ROPE_DIM // 2
    inv = ROPE_THETA ** (-jnp.arange(half, dtype=jnp.float32) * (2.0 / QK_ROPE_DIM))
    ang = jnp.arange(seq_len, dtype=jnp.float32)[:, None] * inv[None, :]
    return jnp.cos(ang), jnp.sin(ang)


def rope(x, cos, sin):
    half = x.shape[-1] // 2
    xf = x.astype(jnp.float32)
    x1, x2 = xf[..., :half], xf[..., half:]
    return jnp.concatenate([x1 * cos - x2 * sin, x2 * cos + x1 * sin], axis=-1).astype(x.dtype)


def causal_block_attention(score_fn, v):
    B, S, H, Dv = v.shape
    n_blocks = S // BLOCK_Q
    k_pos = jnp.arange(S)

    def one_block(blk):
        start = blk * BLOCK_Q
        s = score_fn(start)
        q_pos = start + jnp.arange(BLOCK_Q)
        s = jnp.where(k_pos[None, :] <= q_pos[:, None], s, -jnp.inf)
        w = jax.nn.softmax(s, axis=-1).astype(v.dtype)
        return jnp.einsum('bhqk,bkhd->bqhd', w, v)

    o = lax.map(one_block, jnp.arange(n_blocks))
    return jnp.moveaxis(o, 0, 1).reshape(B, S, H, Dv)


def fox_mixer(x, w_in, b_f, w_o):
    B, S, _ = x.shape
    h = x @ w_in
    q = h[..., :D_MODEL].reshape(B, S, FOX_HEADS, FOX_HEAD_DIM)
    k = h[..., D_MODEL:2 * D_MODEL].reshape(B, S, FOX_HEADS, FOX_HEAD_DIM)
    v = h[..., 2 * D_MODEL:3 * D_MODEL].reshape(B, S, FOX_HEADS, FOX_HEAD_DIM)
    f_logit = h[..., 3 * D_MODEL:].astype(jnp.float32) + b_f.astype(jnp.float32)
    cum = jnp.cumsum(jax.nn.log_sigmoid(f_logit), axis=1).transpose(0, 2, 1)
    scale = FOX_HEAD_DIM ** -0.5

    def scores(start):
        qb = lax.dynamic_slice_in_dim(q, start, BLOCK_Q, axis=1)
        cb = lax.dynamic_slice_in_dim(cum, start, BLOCK_Q, axis=2)
        s = jnp.einsum('bqhd,bkhd->bhqk', qb, k).astype(jnp.float32) * scale
        return s + cb[:, :, :, None] - cum[:, :, None, :]

    o = causal_block_attention(scores, v)
    return o.reshape(B, S, D_MODEL) @ w_o


def shared_kv(x, w_down, kv_norm, w_up, cos, sin):
    B, S, _ = x.shape
    h = x @ w_down
    c_kv = rms_norm(h[..., :KV_LORA_RANK], kv_norm)
    k_rope = rope(h[..., KV_LORA_RANK:], cos, sin)
    kv = jnp.einsum('bsc,chd->bshd', c_kv, w_up)
    return kv[..., :QK_NOPE_DIM], k_rope, kv[..., QK_NOPE_DIM:]


def mla_mixer(x, w_dq, q_norm, w_uq, w_o, k_nope, k_rope, v, cos, sin):
    B, S, _ = x.shape
    c_q = rms_norm(x @ w_dq, q_norm)
    q = jnp.einsum('bsc,chd->bshd', c_q, w_uq)
    q_nope = q[..., :QK_NOPE_DIM]
    q_rope = rope(q[..., QK_NOPE_DIM:], cos[:, None, :], sin[:, None, :])
    scale = (QK_NOPE_DIM + QK_ROPE_DIM) ** -0.5

    def scores(start):
        qn = lax.dynamic_slice_in_dim(q_nope, start, BLOCK_Q, axis=1)
        qr = lax.dynamic_slice_in_dim(q_rope, start, BLOCK_Q, axis=1)
        s = jnp.einsum('bqhd,bkhd->bhqk', qn, k_nope) + jnp.einsum('bqhr,bkr->bhqk', qr, k_rope)
        return s.astype(jnp.float32) * scale

    o = causal_block_attention(scores, v)
    return o.reshape(B, S, MLA_HEADS * V_HEAD_DIM) @ w_o


def setup_inputs(seed: int = 0) -> dict:
    key = jax.random.key(seed)
    ks = iter(jax.random.split(key, 32))
    f32 = jnp.float32

    def nrm(shape, fan_in, scale=1.0):
        return jax.random.normal(next(ks), shape, f32) * (scale * fan_in ** -0.5)

    def gain(shape):
        return 1.0 + 0.02 * jax.random.normal(next(ks), shape, f32)

    def small(shape, s=0.02):
        return s * jax.random.normal(next(ks), shape, f32)

    x = jax.random.normal(next(ks), (BATCH, SEQ, D_MODEL), f32)
    p = jax.random.normal(next(ks), (DEPTH, BATCH, SEQ, PLE_DIM), f32)
    ffn1_w_in = nrm((DEPTH, D_MODEL, 2 * D_FF), D_MODEL)
    ffn1_w_out = nrm((DEPTH, D_FF, D_MODEL), D_FF, BETA)
    ffn2_w_in = nrm((DEPTH, D_MODEL, 2 * D_FF), D_MODEL)
    ffn2_w_out = nrm((DEPTH, D_FF, D_MODEL), D_FF, BETA)
    ln_g = gain((DEPTH, 4, D_MODEL))
    ln_b = small((DEPTH, 4, D_MODEL))
    ple_w_gate = nrm((DEPTH, D_MODEL, D_MODEL), D_MODEL)
    ple_b_gate = small((DEPTH, D_MODEL))
    ple_w_proj = nrm((DEPTH, PLE_DIM, D_MODEL), PLE_DIM, BETA)
    fox_w_in = nrm((N_A, D_MODEL, FOX_IN), D_MODEL)
    fox_w_in = fox_w_in.at[:, :, 2 * D_MODEL:3 * D_MODEL].multiply(BETA)
    fox_b_f = FOX_GATE_BIAS + 0.5 * jax.random.normal(next(ks), (N_A, FOX_HEADS), f32)
    fox_w_o = nrm((N_A, D_MODEL, D_MODEL), D_MODEL, BETA)
    mla_w_dq = nrm((N_B, D_MODEL, Q_LORA_RANK), D_MODEL)
    mla_q_norm = gain((N_B, Q_LORA_RANK))
    mla_w_uq = nrm((N_B, Q_LORA_RANK, MLA_HEADS, QK_NOPE_DIM + QK_ROPE_DIM), Q_LORA_RANK)
    mla_w_o = nrm((N_B, MLA_HEADS * V_HEAD_DIM, D_MODEL), MLA_HEADS * V_HEAD_DIM, BETA)
    kv_w_down = nrm((D_MODEL, KV_LORA_RANK + QK_ROPE_DIM), D_MODEL)
    kv_norm = gain((KV_LORA_RANK,))
    kv_w_up = nrm((KV_LORA_RANK, MLA_HEADS, QK_NOPE_DIM + V_HEAD_DIM), KV_LORA_RANK)
    kv_w_up = kv_w_up.at[:, :, QK_NOPE_DIM:].multiply(BETA)
    return {'x': x, 'p': p,
            'ffn1_w_in': ffn1_w_in, 'ffn1_w_out': ffn1_w_out,
            'ffn2_w_in': ffn2_w_in, 'ffn2_w_out': ffn2_w_out,
            'ln_g': ln_g, 'ln_b': ln_b,
            'ple_w_gate': ple_w_gate, 'ple_b_gate': ple_b_gate, 'ple_w_proj': ple_w_proj,
            'fox_w_in': fox_w_in, 'fox_b_f': fox_b_f, 'fox_w_o': fox_w_o,
            'mla_w_dq': mla_w_dq, 'mla_q_norm': mla_q_norm, 'mla_w_uq': mla_w_uq, 'mla_w_o': mla_w_o,
            'kv_w_down': kv_w_down, 'kv_norm': kv_norm, 'kv_w_up': kv_w_up}


def reference(x, p, ffn1_w_in, ffn1_w_out, ffn2_w_in, ffn2_w_out, ln_g, ln_b,
              ple_w_gate, ple_b_gate, ple_w_proj, fox_w_in, fox_b_f, fox_w_o,
              mla_w_dq, mla_q_norm, mla_w_uq, mla_w_o, kv_w_down, kv_norm, kv_w_up):
    S = x.shape[1]
    cos, sin = rope_tables(S)
    k_nope = k_rope = v_shared = None
    for i in range(DEPTH):
        if i == N_A:
            k_nope, k_rope, v_shared = shared_kv(x, kv_w_down, kv_norm, kv_w_up, cos, sin)
        x = post_norm(x, 0.5 * swiglu(x, ffn1_w_in[i], ffn1_w_out[i]), ln_g[i, 0], ln_b[i, 0])
        if i < N_A:
            mix = fox_mixer(x, fox_w_in[i], fox_b_f[i], fox_w_o[i])
        else:
            j = i - N_A
            mix = mla_mixer(x, mla_w_dq[j], mla_q_norm[j], mla_w_uq[j], mla_w_o[j],
                            k_nope, k_rope, v_shared, cos, sin)
        x = post_norm(x, mix, ln_g[i, 1], ln_b[i, 1])
        x = post_norm(x, 0.5 * swiglu(x, ffn2_w_in[i], ffn2_w_out[i]), ln_g[i, 2], ln_b[i, 2])
        gate = jax.nn.sigmoid((x @ ple_w_gate[i] + ple_b_gate[i]).astype(jnp.float32)).astype(x.dtype)
        x = post_norm(x, gate * (p[i] @ ple_w_proj[i]), ln_g[i, 3], ln_b[i, 3])
    return x
```

```python
import functools

import jax
import jax.numpy as jnp
from jax import lax
from jax.experimental import pallas as pl
from jax.experimental.pallas import tpu as pltpu

BF16 = jnp.bfloat16
F32 = jnp.float32

LN_EPS = 1e-5
RMS_EPS = 1e-6
ROPE_THETA = 10000.0

LANES = 128
VMEM_LIMIT_BYTES = 56 * 1024 * 1024
ROW_TILE = 512
ATTN_TILE = 512
FF_CHUNK = 256
NEG_INIT = -1e30

_NT = (((1,), (1,)), ((), ()))


def _params(n_grid):
    return pltpu.CompilerParams(dimension_semantics=("arbitrary",) * n_grid,
                                vmem_limit_bytes=VMEM_LIMIT_BYTES)


def _const_spec(shape):
    zeros = (0,) * len(shape)
    return pl.BlockSpec(shape, lambda *_: zeros, pipeline_mode=pl.Buffered(1))


def _row_spec(tm, cols):
    return pl.BlockSpec((tm, cols), lambda i: (i, 0))


def _layer_norm(z, g, b):
    mu = jnp.mean(z, axis=-1, keepdims=True)
    zc = z - mu
    var = jnp.mean(zc * zc, axis=-1, keepdims=True)
    return zc * lax.rsqrt(var + LN_EPS) * g + b


def _rms_norm(h, g):
    return h * lax.rsqrt(jnp.mean(h * h, axis=-1, keepdims=True) + RMS_EPS) * g


def _dot(a, b):
    return jnp.dot(a, b, preferred_element_type=F32)


def _ffn_ln_kernel(x_ref, win_ref, wout_ref, g_ref, b_ref, o_ref, act_ref, *, d_ff, alpha):
    x = x_ref[...]
    xb = x.astype(BF16)
    for c in range(d_ff // FF_CHUNK):
        lo = c * FF_CHUNK
        gate = _dot(xb, win_ref[:, lo:lo + FF_CHUNK])
        up = _dot(xb, win_ref[:, d_ff + lo:d_ff + lo + FF_CHUNK])
        act_ref[:, lo:lo + FF_CHUNK] = (gate * jax.nn.sigmoid(gate) * up).astype(BF16)
    y = _dot(act_ref[...], wout_ref[...])
    o_ref[...] = _layer_norm(alpha * x + 0.5 * y, g_ref[...], b_ref[...])


def _ffn_ln(x, w_in, w_out, g, b, alpha):
    m, d = x.shape
    d_ff = w_out.shape[0]
    assert d_ff % FF_CHUNK == 0 and m % ROW_TILE == 0
    tm = ROW_TILE
    return pl.pallas_call(
        functools.partial(_ffn_ln_kernel, d_ff=d_ff, alpha=alpha),
        grid=(m // tm,),
        in_specs=[_row_spec(tm, d), _const_spec(w_in.shape), _const_spec(w_out.shape),
                  _const_spec((1, d)), _const_spec((1, d))],
        out_specs=_row_spec(tm, d),
        out_shape=jax.ShapeDtypeStruct((m, d), F32),
        scratch_shapes=[pltpu.VMEM((tm, d_ff), BF16)],
        compiler_params=_params(1),
        name="ffn_ln",
    )(x, w_in, w_out, g, b)


def _proj_ln_kernel(a_ref, w_ref, x_ref, g_ref, b_ref, o_ref, *, alpha):
    y = _dot(a_ref[...], w_ref[...])
    o_ref[...] = _layer_norm(alpha * x_ref[...] + y, g_ref[...], b_ref[...])


def _proj_ln(a, w, x, g, b, alpha):
    m, d = x.shape
    k = a.shape[1]
    tm = ROW_TILE
    return pl.pallas_call(
        functools.partial(_proj_ln_kernel, alpha=alpha),
        grid=(m // tm,),
        in_specs=[_row_spec(tm, k), _const_spec(w.shape), _row_spec(tm, d),
                  _const_spec((1, d)), _const_spec((1, d))],
        out_specs=_row_spec(tm, d),
        out_shape=jax.ShapeDtypeStruct((m, d), F32),
        compiler_params=_params(1),
        name="proj_ln",
    )(a, w, x, g, b)


def _ple_ln_kernel(x_ref, p_ref, wg_ref, bg_ref, wp_ref, g_ref, b_ref, o_ref, *, alpha):
    x = x_ref[...]
    gate = jax.nn.sigmoid(_dot(x.astype(BF16), wg_ref[...]) + bg_ref[...])
    proj = _dot(p_ref[0].astype(BF16), wp_ref[...])
    o_ref[...] = _layer_norm(alpha * x + gate * proj, g_ref[...], b_ref[...])


def _ple_ln(x, p_all, layer, wg, bg, wp, g, b, alpha):
    m, d = x.shape
    pd = p_all.shape[-1]
    tm = ROW_TILE
    return pl.pallas_call(
        functools.partial(_ple_ln_kernel, alpha=alpha),
        grid=(m // tm,),
        in_specs=[_row_spec(tm, d), pl.BlockSpec((1, tm, pd), lambda i: (layer, i, 0)),
                  _const_spec(wg.shape), _const_spec((1, d)), _const_spec(wp.shape),
                  _const_spec((1, d)), _const_spec((1, d))],
        out_specs=_row_spec(tm, d),
        out_shape=jax.ShapeDtypeStruct((m, d), F32),
        compiler_params=_params(1),
        name="ple_ln",
    )(x, p_all, wg, bg, wp, g, b)


def _fox_proj_kernel(x_ref, wq_ref, wk_ref, wv_ref, wft_ref, q_ref, k_ref, v_ref, ft_ref, *, scale):
    xb = x_ref[0].astype(BF16)
    q_ref[0] = (_dot(xb, wq_ref[...]) * scale).astype(BF16)
    k_ref[0] = _dot(xb, wk_ref[...]).astype(BF16)
    v_ref[0] = _dot(xb, wv_ref[...]).astype(BF16)
    ft_ref[0] = lax.dot_general(wft_ref[...], xb, _NT, preferred_element_type=F32)


def _fox_proj(x3, wq, wk, wv, wft, scale):
    bsz, s, d = x3.shape
    h = wft.shape[0]
    tm = ROW_TILE
    tok = pl.BlockSpec((1, tm, d), lambda b, i: (b, i, 0))
    return pl.pallas_call(
        functools.partial(_fox_proj_kernel, scale=scale),
        grid=(bsz, s // tm),
        in_specs=[tok, _const_spec(wq.shape), _const_spec(wk.shape), _const_spec(wv.shape),
                  _const_spec(wft.shape)],
        out_specs=[tok, tok, tok, pl.BlockSpec((1, h, tm), lambda b, i: (b, 0, i))],
        out_shape=[jax.ShapeDtypeStruct((bsz, s, d), BF16)] * 3
        + [jax.ShapeDtypeStruct((bsz, h, s), F32)],
        compiler_params=_params(2),
        name="fox_proj",
    )(x3, wq, wk, wv, wft)


def _fox_decay_kernel(f_ref, bf_ref, o_ref):
    f = f_ref[0] + bf_ref[...]
    c = jnp.minimum(f, 0.0) - jnp.log1p(jnp.exp(-jnp.abs(f)))
    lane = lax.broadcasted_iota(jnp.int32, c.shape, 1)
    shift = 1
    while shift < c.shape[1]:
        c = c + jnp.where(lane >= shift, pltpu.roll(c, shift, axis=1), 0.0)
        shift *= 2
    o_ref[0] = -c


def _fox_decay(ft, b_f):
    bsz, h, s = ft.shape
    blk = pl.BlockSpec((1, h, s), lambda b: (b, 0, 0))
    return pl.pallas_call(
        _fox_decay_kernel,
        grid=(bsz,),
        in_specs=[blk, _const_spec((h, 1))],
        out_specs=blk,
        out_shape=jax.ShapeDtypeStruct((bsz, h, s), F32),
        compiler_params=_params(1),
        name="fox_decay",
    )(ft, b_f)


def _kv_shared_kernel(x_ref, wd_ref, g_ref, wk_ref, wv_ref, c1_ref, c2_ref,
                      kn_ref, v_ref, kr_ref, *, rank):
    xb = x_ref[...].astype(BF16)
    h = _dot(xb, wd_ref[...])
    c = _rms_norm(h[:, :rank], g_ref[...]).astype(BF16)
    kn_ref[...] = _dot(c, wk_ref[...]).astype(BF16)
    v_ref[...] = _dot(c, wv_ref[...]).astype(BF16)
    hr = h[:, rank:]
    kr = hr * c1_ref[...] + pltpu.roll(hr, LANES // 2, axis=1) * c2_ref[...]
    kr_ref[...] = kr.astype(BF16)


def _kv_shared(x, wd, g, wk, wv, c1, c2, rank):
    m, d = x.shape
    s = c1.shape[0]
    tm = ROW_TILE
    nblk = s // tm
    tab = pl.BlockSpec((tm, LANES), lambda i: (i % nblk, 0))
    hv = wk.shape[1]
    return pl.pallas_call(
        functools.partial(_kv_shared_kernel, rank=rank),
        grid=(m // tm,),
        in_specs=[_row_spec(tm, d), _const_spec(wd.shape), _const_spec((1, rank)),
                  _const_spec(wk.shape), _const_spec(wv.shape), tab, tab],
        out_specs=[_row_spec(tm, hv), _row_spec(tm, hv), _row_spec(tm, LANES)],
        out_shape=[jax.ShapeDtypeStruct((m, hv), BF16), jax.ShapeDtypeStruct((m, hv), BF16),
                   jax.ShapeDtypeStruct((m, LANES), BF16)],
        compiler_params=_params(1),
        name="kv_shared",
    )(x, wd, g, wk, wv, c1, c2)


def _mla_q_kernel(x_ref, wdq_ref, g_ref, wn_ref, wr_ref, wrr_ref, c1_ref, c2_ref, q_ref,
                  *, heads, scale):
    xb = x_ref[...].astype(BF16)
    cq = _rms_norm(_dot(xb, wdq_ref[...]), g_ref[...]).astype(BF16)
    qn = _dot(cq, wn_ref[...])
    qr = _dot(cq, wr_ref[...])
    qrr = _dot(cq, wrr_ref[...])
    c1 = c1_ref[...]
    c2 = c2_ref[...]
    for h in range(heads):
        lo = h * LANES
        q_ref[:, 2 * lo:2 * lo + LANES] = (qn[:, lo:lo + LANES] * scale).astype(BF16)
        roped = qr[:, lo:lo + LANES] * c1 + qrr[:, lo:lo + LANES] * c2
        q_ref[:, 2 * lo + LANES:2 * lo + 2 * LANES] = (roped * scale).astype(BF16)


def _mla_q(x, wdq, g, wn, wr, wrr, c1, c2, heads, scale):
    m, d = x.shape
    s = c1.shape[0]
    tm = ROW_TILE
    nblk = s // tm
    tab = pl.BlockSpec((tm, LANES), lambda i: (i % nblk, 0))
    qw = heads * 2 * LANES
    return pl.pallas_call(
        functools.partial(_mla_q_kernel, heads=heads, scale=scale),
        grid=(m // tm,),
        in_specs=[_row_spec(tm, d), _const_spec(wdq.shape), _const_spec((1, wdq.shape[1])),
                  _const_spec(wn.shape), _const_spec(wr.shape), _const_spec(wrr.shape), tab, tab],
        out_specs=_row_spec(tm, qw),
        out_shape=jax.ShapeDtypeStruct((m, qw), BF16),
        compiler_params=_params(1),
        name="mla_q",
    )(x, wdq, g, wn, wr, wrr, c1, c2)


def _softmax_step(s, v, m_ref, l_ref, acc_ref, idx):
    reps = s.shape[1] // LANES
    m_prev = m_ref[idx]
    m_next = jnp.maximum(m_prev, jnp.max(s, axis=1, keepdims=True))
    alpha = jnp.exp(m_prev - m_next)
    p = jnp.exp(s - jnp.concatenate([m_next] * reps, axis=1))
    l_ref[idx] = alpha * l_ref[idx] + jnp.sum(p, axis=1, keepdims=True)
    acc_ref[idx] = alpha * acc_ref[idx] + _dot(p.astype(BF16), v)
    m_ref[idx] = m_next


def _causal_mask(s):
    row = lax.broadcasted_iota(jnp.int32, s.shape, 0)
    col = lax.broadcasted_iota(jnp.int32, s.shape, 1)
    return jnp.where(col <= row, s, -jnp.inf)


def _fox_attn_kernel(q_ref, k_ref, v_ref, nc_ref, o_ref, m_ref, l_ref, acc_ref, *, t, hd):
    qi = pl.program_id(2)
    q = q_ref[0]
    lane = lax.broadcasted_iota(jnp.int32, q.shape, 1)
    zero = jnp.zeros_like(q)
    q_heads = (jnp.where(lane < hd, q, zero), jnp.where(lane >= hd, q, zero))
    m_ref[...] = jnp.full(m_ref.shape, NEG_INIT, F32)
    l_ref[...] = jnp.zeros(l_ref.shape, F32)
    acc_ref[...] = jnp.zeros(acc_ref.shape, F32)

    def block(j, masked):
        start = pl.multiple_of(j * t, t)
        k = k_ref[0, pl.ds(start, t), :]
        v = v_ref[0, pl.ds(start, t), :]
        nc = nc_ref[0, 0, j]
        for h in range(2):
            s = lax.dot_general(q_heads[h], k, _NT, preferred_element_type=F32) + nc[h:h + 1, :]
            if masked:
                s = _causal_mask(s)
            _softmax_step(s, v, m_ref, l_ref, acc_ref, h)

    def body(j, carry):
        block(j, False)
        return carry

    lax.fori_loop(0, qi, body, 0)
    block(qi, True)
    o = jnp.where(lane < hd, acc_ref[0] / l_ref[0], acc_ref[1] / l_ref[1])
    o_ref[0] = o.astype(o_ref.dtype)


def _fox_attn(q, k, v, negcum, hd):
    bsz, s, d = q.shape
    t = ATTN_TILE
    assert 2 * hd == LANES and s % t == 0
    pairs = d // LANES
    nblk = s // t
    nc = negcum.reshape(bsz, pairs, 2, nblk, t).transpose(0, 1, 3, 2, 4)
    qspec = pl.BlockSpec((1, t, LANES), lambda b, p, i: (b, i, p))
    kvspec = pl.BlockSpec((1, s, LANES), lambda b, p, i: (b, 0, p))
    return pl.pallas_call(
        functools.partial(_fox_attn_kernel, t=t, hd=hd),
        grid=(bsz, pairs, nblk),
        in_specs=[qspec, kvspec, kvspec,
                  pl.BlockSpec((1, 1, nblk, 2, t), lambda b, p, i: (b, p, 0, 0, 0))],
        out_specs=qspec,
        out_shape=jax.ShapeDtypeStruct((bsz, s, d), BF16),
        scratch_shapes=[pltpu.VMEM((2, t, LANES), F32)] * 3,
        compiler_params=_params(3),
        name="fox_attn",
    )(q, k, v, nc)


def _mla_attn_kernel(q_ref, kn_ref, kr_ref, v_ref, o_ref, kf_ref, m_ref, l_ref, acc_ref, *, t):
    qi = pl.program_id(2)

    @pl.when(qi == 0)
    def _():
        kf_ref[:, :LANES] = kn_ref[0]
        kf_ref[:, LANES:] = kr_ref[0]

    q = q_ref[0]
    m_ref[...] = jnp.full(m_ref.shape, NEG_INIT, F32)
    l_ref[...] = jnp.zeros(l_ref.shape, F32)
    acc_ref[...] = jnp.zeros(acc_ref.shape, F32)

    def block(j, masked):
        start = pl.multiple_of(j * t, t)
        k = kf_ref[pl.ds(start, t), :]
        v = v_ref[0, pl.ds(start, t), :]
        s = lax.dot_general(q, k, _NT, preferred_element_type=F32)
        if masked:
            s = _causal_mask(s)
        _softmax_step(s, v, m_ref, l_ref, acc_ref, 0)

    def body(j, carry):
        block(j, False)
        return carry

    lax.fori_loop(0, qi, body, 0)
    block(qi, True)
    o_ref[0] = (acc_ref[0] / l_ref[0]).astype(o_ref.dtype)


def _mla_attn(q, kn, kr, v, heads):
    bsz, s, _ = q.shape
    t = ATTN_TILE
    assert s % t == 0
    nblk = s // t
    return pl.pallas_call(
        functools.partial(_mla_attn_kernel, t=t),
        grid=(bsz, heads, nblk),
        in_specs=[pl.BlockSpec((1, t, 2 * LANES), lambda b, h, i: (b, i, h)),
                  pl.BlockSpec((1, s, LANES), lambda b, h, i: (b, 0, h)),
                  pl.BlockSpec((1, s, LANES), lambda b, h, i: (b, 0, 0)),
                  pl.BlockSpec((1, s, LANES), lambda b, h, i: (b, 0, h))],
        out_specs=pl.BlockSpec((1, t, LANES), lambda b, h, i: (b, i, h)),
        out_shape=jax.ShapeDtypeStruct((bsz, s, heads * LANES), BF16),
        scratch_shapes=[pltpu.VMEM((s, 2 * LANES), BF16)] + [pltpu.VMEM((1, t, LANES), F32)] * 3,
        compiler_params=_params(3),
        name="mla_attn",
    )(q, kn, kr, v)


def _rot_half_cols(w):
    half = w.shape[-1] // 2
    return jnp.concatenate([-w[..., half:], w[..., :half]], axis=-1)


def kernel(x, p, ffn1_w_in, ffn1_w_out, ffn2_w_in, ffn2_w_out, ln_g, ln_b, ple_w_gate, ple_b_gate, ple_w_proj, fox_w_in, fox_b_f, fox_w_o, mla_w_dq, mla_q_norm, mla_w_uq, mla_w_o, kv_w_down, kv_norm, kv_w_up):
    bsz, s, d = x.shape
    depth = ffn1_w_in.shape[0]
    n_a = fox_w_in.shape[0]
    m = bsz * s
    alpha = (2 * depth) ** 0.25
    fox_heads = fox_b_f.shape[1]
    fox_hd = d // fox_heads
    rank, mla_heads = kv_w_up.shape[0], kv_w_up.shape[1]
    rope_dim = kv_w_down.shape[1] - rank
    nope = mla_w_uq.shape[3] - rope_dim
    assert nope == LANES and kv_w_up.shape[2] == 2 * LANES and 2 * rope_dim == LANES

    half = rope_dim // 2
    inv = ROPE_THETA ** (-jnp.arange(half, dtype=F32) * (2.0 / rope_dim))
    ang = jnp.arange(s, dtype=F32)[:, None] * inv[None, :]
    pad = jnp.zeros((s, LANES - rope_dim), F32)
    c1 = jnp.concatenate([jnp.cos(ang), jnp.cos(ang), pad], axis=1)
    c2 = jnp.concatenate([jnp.sin(ang), jnp.sin(ang), pad], axis=1)

    p2 = p.reshape(depth, m, p.shape[-1])
    row = lambda a: a.reshape(1, -1)
    xs = x.reshape(m, d)
    kn = kr = vs = None
    for i in range(depth):
        if i == n_a:
            wdr = kv_w_down[:, rank:]
            wd = jnp.concatenate([kv_w_down[:, :rank], wdr, _rot_half_cols(wdr)], axis=1).astype(BF16)
            wk = kv_w_up[:, :, :nope].reshape(rank, -1).astype(BF16)
            wv = kv_w_up[:, :, nope:].reshape(rank, -1).astype(BF16)
            kn, vs, kr = _kv_shared(xs, wd, row(kv_norm), wk, wv, c1, c2, rank)
            kn, vs, kr = (a.reshape(bsz, s, -1) for a in (kn, vs, kr))
        xs = _ffn_ln(xs, ffn1_w_in[i].astype(BF16), ffn1_w_out[i].astype(BF16),
                     row(ln_g[i, 0]), row(ln_b[i, 0]), alpha)
        if i < n_a:
            w = fox_w_in[i]
            q, k, v, ft = _fox_proj(xs.reshape(bsz, s, d), w[:, :d].astype(BF16),
                                    w[:, d:2 * d].astype(BF16), w[:, 2 * d:3 * d].astype(BF16),
                                    w[:, 3 * d:].T.astype(BF16), fox_hd ** -0.5)
            negcum = _fox_decay(ft, fox_b_f[i].reshape(-1, 1))
            o = _fox_attn(q, k, v, negcum, fox_hd)
            w_o = fox_w_o[i]
        else:
            j = i - n_a
            wuq = mla_w_uq[j]
            q_rank = wuq.shape[0]
            wn = wuq[:, :, :nope].reshape(q_rank, -1).astype(BF16)
            wr3 = wuq[:, :, nope:]
            zpad = ((0, 0), (0, 0), (0, LANES - rope_dim))
            wr = jnp.pad(wr3, zpad).reshape(q_rank, -1).astype(BF16)
            wrr = jnp.pad(_rot_half_cols(wr3), zpad).reshape(q_rank, -1).astype(BF16)
            q = _mla_q(xs, mla_w_dq[j].astype(BF16), row(mla_q_norm[j]), wn, wr, wrr, c1, c2,
                       mla_heads, (nope + rope_dim) ** -0.5)
            o = _mla_attn(q.reshape(bsz, s, -1), kn, kr, vs, mla_heads)
            w_o = mla_w_o[j]
        xs = _proj_ln(o.reshape(m, -1), w_o.astype(BF16), xs, row(ln_g[i, 1]), row(ln_b[i, 1]), alpha)
        xs = _ffn_ln(xs, ffn2_w_in[i].astype(BF16), ffn2_w_out[i].astype(BF16),
                     row(ln_g[i, 2]), row(ln_b[i, 2]), alpha)
        xs = _ple_ln(xs, p2, i, ple_w_gate[i].astype(BF16), row(ple_b_gate[i]),
                     ple_w_proj[i].astype(BF16), row(ln_g[i, 3]), row(ln_b[i, 3]), alpha)
    return xs.reshape(bsz, s, d)
```

```python
import functools

import jax
import jax.numpy as jnp
from jax import lax
from jax.experimental import pallas as pl
from jax.experimental.pallas import tpu as pltpu

BF16 = jnp.bfloat16
F32 = jnp.float32

LN_EPS = 1e-5
RMS_EPS = 1e-6
ROPE_THETA = 10000.0

LANES = 128
VMEM_LIMIT_BYTES = 56 * 1024 * 1024
ROW_TILE = 512
ATTN_TILE = 512
FF_CHUNK = 256
SOFTMAX_ROWS = 32
NEG_INIT = -1e30
LOG2E = 1.4426950408889634

_NT = (((1,), (1,)), ((), ()))


def _params(n_grid):
    return pltpu.CompilerParams(dimension_semantics=("arbitrary",) * n_grid,
                                vmem_limit_bytes=VMEM_LIMIT_BYTES)


def _const_spec(shape):
    zeros = (0,) * len(shape)
    return pl.BlockSpec(shape, lambda *_: zeros, pipeline_mode=pl.Buffered(1))


def _row_spec(tm, cols):
    return pl.BlockSpec((tm, cols), lambda i: (i, 0))


def _layer_norm(z, g, b):
    mu = jnp.mean(z, axis=-1, keepdims=True)
    zc = z - mu
    var = jnp.mean(zc * zc, axis=-1, keepdims=True)
    return zc * lax.rsqrt(var + LN_EPS) * g + b


def _rms_norm(h, g):
    return h * lax.rsqrt(jnp.mean(h * h, axis=-1, keepdims=True) + RMS_EPS) * g


def _dot(a, b):
    return jnp.dot(a, b, preferred_element_type=F32)


def _ffn_ln_kernel(x_ref, win_ref, wout_ref, g_ref, b_ref, o_ref, act_ref, *, d_ff, alpha):
    x = x_ref[...]
    xb = x.astype(BF16)
    for c in range(d_ff // FF_CHUNK):
        lo = c * FF_CHUNK
        gate = _dot(xb, win_ref[:, lo:lo + FF_CHUNK])
        up = _dot(xb, win_ref[:, d_ff + lo:d_ff + lo + FF_CHUNK])
        act_ref[:, lo:lo + FF_CHUNK] = (gate * jax.nn.sigmoid(gate) * up).astype(BF16)
    y = _dot(act_ref[...], wout_ref[...])
    o_ref[...] = _layer_norm(alpha * x + 0.5 * y, g_ref[...], b_ref[...])


def _ffn_ln(x, w_in, w_out, g, b, alpha):
    m, d = x.shape
    d_ff = w_out.shape[0]
    assert d_ff % FF_CHUNK == 0 and m % ROW_TILE == 0
    tm = ROW_TILE
    return pl.pallas_call(
        functools.partial(_ffn_ln_kernel, d_ff=d_ff, alpha=alpha),
        grid=(m // tm,),
        in_specs=[_row_spec(tm, d), _const_spec(w_in.shape), _const_spec(w_out.shape),
                  _const_spec((1, d)), _const_spec((1, d))],
        out_specs=_row_spec(tm, d),
        out_shape=jax.ShapeDtypeStruct((m, d), F32),
        scratch_shapes=[pltpu.VMEM((tm, d_ff), BF16)],
        compiler_params=_params(1),
        name="ffn_ln",
    )(x, w_in, w_out, g, b)


def _proj_ln_kernel(a_ref, w_ref, x_ref, g_ref, b_ref, o_ref, *, alpha):
    y = _dot(a_ref[...], w_ref[...])
    o_ref[...] = _layer_norm(alpha * x_ref[...] + y, g_ref[...], b_ref[...])


def _proj_ln(a, w, x, g, b, alpha):
    m, d = x.shape
    k = a.shape[1]
    tm = ROW_TILE
    return pl.pallas_call(
        functools.partial(_proj_ln_kernel, alpha=alpha),
        grid=(m // tm,),
        in_specs=[_row_spec(tm, k), _const_spec(w.shape), _row_spec(tm, d),
                  _const_spec((1, d)), _const_spec((1, d))],
        out_specs=_row_spec(tm, d),
        out_shape=jax.ShapeDtypeStruct((m, d), F32),
        compiler_params=_params(1),
        name="proj_ln",
    )(a, w, x, g, b)


def _ple_ln_kernel(x_ref, p_ref, wg_ref, bg_ref, wp_ref, g_ref, b_ref, o_ref, *, alpha):
    x = x_ref[...]
    gate = jax.nn.sigmoid(_dot(x.astype(BF16), wg_ref[...]) + bg_ref[...])
    proj = _dot(p_ref[0].astype(BF16), wp_ref[...])
    o_ref[...] = _layer_norm(alpha * x + gate * proj, g_ref[...], b_ref[...])


def _ple_ln(x, p_all, layer, wg, bg, wp, g, b, alpha):
    m, d = x.shape
    pd = p_all.shape[-1]
    tm = ROW_TILE
    return pl.pallas_call(
        functools.partial(_ple_ln_kernel, alpha=alpha),
        grid=(m // tm,),
        in_specs=[_row_spec(tm, d), pl.BlockSpec((1, tm, pd), lambda i: (layer, i, 0)),
                  _const_spec(wg.shape), _const_spec((1, d)), _const_spec(wp.shape),
                  _const_spec((1, d)), _const_spec((1, d))],
        out_specs=_row_spec(tm, d),
        out_shape=jax.ShapeDtypeStruct((m, d), F32),
        compiler_params=_params(1),
        name="ple_ln",
    )(x, p_all, wg, bg, wp, g, b)


def _fox_proj_kernel(x_ref, wq_ref, wk_ref, wv_ref, wft_ref, q_ref, k_ref, v_ref, ft_ref, *, scale):
    xb = x_ref[0].astype(BF16)
    q_ref[0] = (_dot(xb, wq_ref[...]) * scale).astype(BF16)
    k_ref[0] = _dot(xb, wk_ref[...]).astype(BF16)
    v_ref[0] = _dot(xb, wv_ref[...]).astype(BF16)
    ft_ref[0] = lax.dot_general(wft_ref[...], xb, _NT, preferred_element_type=F32)


def _fox_proj(x3, wq, wk, wv, wft, scale):
    bsz, s, d = x3.shape
    h = wft.shape[0]
    tm = ROW_TILE
    tok = pl.BlockSpec((1, tm, d), lambda b, i: (b, i, 0))
    return pl.pallas_call(
        functools.partial(_fox_proj_kernel, scale=scale),
        grid=(bsz, s // tm),
        in_specs=[tok, _const_spec(wq.shape), _const_spec(wk.shape), _const_spec(wv.shape),
                  _const_spec(wft.shape)],
        out_specs=[tok, tok, tok, pl.BlockSpec((1, h, tm), lambda b, i: (b, 0, i))],
        out_shape=[jax.ShapeDtypeStruct((bsz, s, d), BF16)] * 3
        + [jax.ShapeDtypeStruct((bsz, h, s), F32)],
        compiler_params=_params(2),
        name="fox_proj",
    )(x3, wq, wk, wv, wft)


def _fox_decay_kernel(f_ref, bf_ref, o_ref):
    f = f_ref[0] + bf_ref[...]
    c = jnp.minimum(f, 0.0) - jnp.log1p(jnp.exp(-jnp.abs(f)))
    lane = lax.broadcasted_iota(jnp.int32, c.shape, 1)
    shift = 1
    while shift < c.shape[1]:
        c = c + jnp.where(lane >= shift, pltpu.roll(c, shift, axis=1), 0.0)
        shift *= 2
    o_ref[0] = -LOG2E * c


def _fox_decay(ft, b_f):
    bsz, h, s = ft.shape
    blk = pl.BlockSpec((1, h, s), lambda b: (b, 0, 0))
    return pl.pallas_call(
        _fox_decay_kernel,
        grid=(bsz,),
        in_specs=[blk, _const_spec((h, 1))],
        out_specs=blk,
        out_shape=jax.ShapeDtypeStruct((bsz, h, s), F32),
        compiler_params=_params(1),
        name="fox_decay",
    )(ft, b_f)


def _kv_shared_kernel(x_ref, wd_ref, g_ref, wk_ref, wv_ref, c1_ref, c2_ref,
                      kn_ref, v_ref, kr_ref, *, rank):
    xb = x_ref[...].astype(BF16)
    h = _dot(xb, wd_ref[...])
    c = _rms_norm(h[:, :rank], g_ref[...]).astype(BF16)
    kn_ref[...] = _dot(c, wk_ref[...]).astype(BF16)
    v_ref[...] = _dot(c, wv_ref[...]).astype(BF16)
    hr = h[:, rank:]
    kr = hr * c1_ref[...] + pltpu.roll(hr, LANES // 2, axis=1) * c2_ref[...]
    kr_ref[...] = kr.astype(BF16)


def _kv_shared(x, wd, g, wk, wv, c1, c2, rank):
    m, d = x.shape
    s = c1.shape[0]
    tm = ROW_TILE
    nblk = s // tm
    tab = pl.BlockSpec((tm, LANES), lambda i: (i % nblk, 0))
    hv = wk.shape[1]
    return pl.pallas_call(
        functools.partial(_kv_shared_kernel, rank=rank),
        grid=(m // tm,),
        in_specs=[_row_spec(tm, d), _const_spec(wd.shape), _const_spec((1, rank)),
                  _const_spec(wk.shape), _const_spec(wv.shape), tab, tab],
        out_specs=[_row_spec(tm, hv), _row_spec(tm, hv), _row_spec(tm, LANES)],
        out_shape=[jax.ShapeDtypeStruct((m, hv), BF16), jax.ShapeDtypeStruct((m, hv), BF16),
                   jax.ShapeDtypeStruct((m, LANES), BF16)],
        compiler_params=_params(1),
        name="kv_shared",
    )(x, wd, g, wk, wv, c1, c2)


def _mla_q_kernel(x_ref, wdq_ref, g_ref, wn_ref, wr_ref, wrr_ref, c1_ref, c2_ref, q_ref,
                  *, heads, scale):
    xb = x_ref[...].astype(BF16)
    cq = _rms_norm(_dot(xb, wdq_ref[...]), g_ref[...]).astype(BF16)
    qn = _dot(cq, wn_ref[...])
    qr = _dot(cq, wr_ref[...])
    qrr = _dot(cq, wrr_ref[...])
    c1 = c1_ref[...]
    c2 = c2_ref[...]
    for h in range(heads):
        lo = h * LANES
        q_ref[:, 2 * lo:2 * lo + LANES] = (qn[:, lo:lo + LANES] * scale).astype(BF16)
        roped = qr[:, lo:lo + LANES] * c1 + qrr[:, lo:lo + LANES] * c2
        q_ref[:, 2 * lo + LANES:2 * lo + 2 * LANES] = (roped * scale).astype(BF16)


def _mla_q(x, wdq, g, wn, wr, wrr, c1, c2, heads, scale):
    m, d = x.shape
    s = c1.shape[0]
    tm = ROW_TILE
    nblk = s // tm
    tab = pl.BlockSpec((tm, LANES), lambda i: (i % nblk, 0))
    qw = heads * 2 * LANES
    return pl.pallas_call(
        functools.partial(_mla_q_kernel, heads=heads, scale=scale),
        grid=(m // tm,),
        in_specs=[_row_spec(tm, d), _const_spec(wdq.shape), _const_spec((1, wdq.shape[1])),
                  _const_spec(wn.shape), _const_spec(wr.shape), _const_spec(wrr.shape), tab, tab],
        out_specs=_row_spec(tm, qw),
        out_shape=jax.ShapeDtypeStruct((m, qw), BF16),
        compiler_params=_params(1),
        name="mla_q",
    )(x, wdq, g, wn, wr, wrr, c1, c2)


def _softmax_rows(s_ref, p_ref, m_ref, l_ref, a_ref, idx, masked):
    t, tk = s_ref.shape
    reps = tk // LANES
    for r in range(0, t, SOFTMAX_ROWS):
        rows = pl.ds(r, SOFTMAX_ROWS)
        live = min(tk, -(-(r + SOFTMAX_ROWS) // LANES) * LANES) if masked else tk
        s = s_ref[rows, :live]
        if masked:
            row = r + lax.broadcasted_iota(jnp.int32, s.shape, 0)
            col = lax.broadcasted_iota(jnp.int32, s.shape, 1)
            s = jnp.where(col <= row, s, -jnp.inf)
        m_prev = m_ref[idx, rows, :]
        m_next = jnp.maximum(m_prev, jnp.max(s, axis=1, keepdims=True))
        alpha = jnp.exp2(m_prev - m_next)
        p = jnp.exp2(s - jnp.concatenate([m_next] * (live // LANES), axis=1))
        l_ref[idx, rows, :] = alpha * l_ref[idx, rows, :] + jnp.sum(p, axis=1, keepdims=True)
        m_ref[idx, rows, :] = m_next
        a_ref[rows, :] = alpha
        p_ref[rows, :live] = p.astype(BF16)
        if live < tk:
            p_ref[rows, live:] = jnp.zeros((SOFTMAX_ROWS, tk - live), BF16)


def _causal_pipeline(n_tiles, n_heads, logits, softmax, values):
    for k in range(n_tiles + 2):
        for h in range(n_heads):
            if 0 <= k - 2:
                values(h, k - 2, k % 2)
            if k < n_tiles:
                logits(h, k, k % 2)
            if 0 <= k - 1 < n_tiles:
                softmax(h, k - 1, (k - 1) % 2, k - 1 == n_tiles - 1)


def _attn_scratch(t, n_heads):
    return ([pltpu.VMEM((2 * n_heads, t, t), F32), pltpu.VMEM((2 * n_heads, t, t), BF16),
             pltpu.VMEM((2 * n_heads, t, LANES), F32)] + [pltpu.VMEM((n_heads, t, LANES), F32)] * 3)


def _fox_attn_kernel(q_ref, k_ref, v_ref, nc_ref, o_ref, s_ref, p_ref, a_ref, m_ref, l_ref, acc_ref,
                     *, t, hd, n_q):
    qi = pl.program_id(2)
    lane = lax.broadcasted_iota(jnp.int32, (t, LANES), 1)

    def logits(h, j, slot):
        q = q_ref[0]
        q_h = jnp.where((lane < hd) if h == 0 else (lane >= hd), q, jnp.zeros_like(q))
        s = lax.dot_general(q_h, k_ref[0, j * t:(j + 1) * t, :], _NT, preferred_element_type=F32)
        s_ref[2 * h + slot] = s + nc_ref[0, 0, h:h + 1, j * t:(j + 1) * t]

    def softmax(h, j, slot, masked):
        i = 2 * h + slot
        _softmax_rows(s_ref.at[i], p_ref.at[i], m_ref, l_ref, a_ref.at[i], h, masked)

    def values(h, j, slot):
        i = 2 * h + slot
        acc_ref[h] = a_ref[i] * acc_ref[h] + _dot(p_ref[i], v_ref[0, j * t:(j + 1) * t, :])

    for c in range(n_q):
        @pl.when(qi == c)
        def _(c=c):
            m_ref[...] = jnp.full(m_ref.shape, NEG_INIT, F32)
            l_ref[...] = jnp.zeros(l_ref.shape, F32)
            acc_ref[...] = jnp.zeros(acc_ref.shape, F32)
            _causal_pipeline(c + 1, 2, logits, softmax, values)
            o = jnp.where(lane < hd, acc_ref[0] / l_ref[0], acc_ref[1] / l_ref[1])
            o_ref[0] = o.astype(o_ref.dtype)


def _fox_attn(q, k, v, negcum, hd):
    bsz, s, d = q.shape
    t = ATTN_TILE
    assert 2 * hd == LANES and s % t == 0
    pairs = d // LANES
    nblk = s // t
    qspec = pl.BlockSpec((1, t, LANES), lambda b, p, i: (b, i, p))
    kvspec = pl.BlockSpec((1, s, LANES), lambda b, p, i: (b, 0, p))
    return pl.pallas_call(
        functools.partial(_fox_attn_kernel, t=t, hd=hd, n_q=nblk),
        grid=(bsz, pairs, nblk),
        in_specs=[qspec, kvspec, kvspec,
                  pl.BlockSpec((1, 1, 2, s), lambda b, p, i: (b, p, 0, 0))],
        out_specs=qspec,
        out_shape=jax.ShapeDtypeStruct((bsz, s, d), BF16),
        scratch_shapes=_attn_scratch(t, 2),
        compiler_params=_params(3),
        name="fox_attn",
    )(q, k, v, negcum.reshape(bsz, pairs, 2, s))


def _mla_attn_kernel(q_ref, kn_ref, kr_ref, v_ref, o_ref, kf_ref, s_ref, p_ref, a_ref,
                     m_ref, l_ref, acc_ref, *, t, n_q):
    qi = pl.program_id(2)

    @pl.when(qi == 0)
    def _():
        kf_ref[:, :LANES] = kn_ref[0]
        kf_ref[:, LANES:] = kr_ref[0]

    def logits(h, j, slot):
        s_ref[slot] = lax.dot_general(q_ref[0], kf_ref[j * t:(j + 1) * t, :], _NT,
                                      preferred_element_type=F32)

    def softmax(h, j, slot, masked):
        _softmax_rows(s_ref.at[slot], p_ref.at[slot], m_ref, l_ref, a_ref.at[slot], 0, masked)

    def values(h, j, slot):
        acc_ref[0] = a_ref[slot] * acc_ref[0] + _dot(p_ref[slot], v_ref[0, j * t:(j + 1) * t, :])

    for c in range(n_q):
        @pl.when(qi == c)
        def _(c=c):
            m_ref[...] = jnp.full(m_ref.shape, NEG_INIT, F32)
            l_ref[...] = jnp.zeros(l_ref.shape, F32)
            acc_ref[...] = jnp.zeros(acc_ref.shape, F32)
            _causal_pipeline(c + 1, 1, logits, softmax, values)
            o_ref[0] = (acc_ref[0] / l_ref[0]).astype(o_ref.dtype)


def _mla_attn(q, kn, kr, v, heads):
    bsz, s, _ = q.shape
    t = ATTN_TILE
    assert s % t == 0
    nblk = s // t
    return pl.pallas_call(
        functools.partial(_mla_attn_kernel, t=t, n_q=nblk),
        grid=(bsz, heads, nblk),
        in_specs=[pl.BlockSpec((1, t, 2 * LANES), lambda b, h, i: (b, i, h)),
                  pl.BlockSpec((1, s, LANES), lambda b, h, i: (b, 0, h)),
                  pl.BlockSpec((1, s, LANES), lambda b, h, i: (b, 0, 0)),
                  pl.BlockSpec((1, s, LANES), lambda b, h, i: (b, 0, h))],
        out_specs=pl.BlockSpec((1, t, LANES), lambda b, h, i: (b, i, h)),
        out_shape=jax.ShapeDtypeStruct((bsz, s, heads * LANES), BF16),
        scratch_shapes=[pltpu.VMEM((s, 2 * LANES), BF16)] + _attn_scratch(t, 1),
        compiler_params=_params(3),
        name="mla_attn",
    )(q, kn, kr, v)


def _rot_half_cols(w):
    half = w.shape[-1] // 2
    return jnp.concatenate([-w[..., half:], w[..., :half]], axis=-1)


def kernel(x, p, ffn1_w_in, ffn1_w_out, ffn2_w_in, ffn2_w_out, ln_g, ln_b, ple_w_gate, ple_b_gate, ple_w_proj, fox_w_in, fox_b_f, fox_w_o, mla_w_dq, mla_q_norm, mla_w_uq, mla_w_o, kv_w_down, kv_norm, kv_w_up):
    bsz, s, d = x.shape
    depth = ffn1_w_in.shape[0]
    n_a = fox_w_in.shape[0]
    m = bsz * s
    alpha = (2 * depth) ** 0.25
    fox_heads = fox_b_f.shape[1]
    fox_hd = d // fox_heads
    rank, mla_heads = kv_w_up.shape[0], kv_w_up.shape[1]
    rope_dim = kv_w_down.shape[1] - rank
    nope = mla_w_uq.shape[3] - rope_dim
    assert nope == LANES and kv_w_up.shape[2] == 2 * LANES and 2 * rope_dim == LANES

    half = rope_dim // 2
    inv = ROPE_THETA ** (-jnp.arange(half, dtype=F32) * (2.0 / rope_dim))
    ang = jnp.arange(s, dtype=F32)[:, None] * inv[None, :]
    pad = jnp.zeros((s, LANES - rope_dim), F32)
    c1 = jnp.concatenate([jnp.cos(ang), jnp.cos(ang), pad], axis=1)
    c2 = jnp.concatenate([jnp.sin(ang), jnp.sin(ang), pad], axis=1)

    p2 = p.reshape(depth, m, p.shape[-1])
    row = lambda a: a.reshape(1, -1)
    xs = x.reshape(m, d)
    kn = kr = vs = None
    for i in range(depth):
        if i == n_a:
            wdr = kv_w_down[:, rank:]
            wd = jnp.concatenate([kv_w_down[:, :rank], wdr, _rot_half_cols(wdr)], axis=1).astype(BF16)
            wk = kv_w_up[:, :, :nope].reshape(rank, -1).astype(BF16)
            wv = kv_w_up[:, :, nope:].reshape(rank, -1).astype(BF16)
            kn, vs, kr = _kv_shared(xs, wd, row(kv_norm), wk, wv, c1, c2, rank)
            kn, vs, kr = (a.reshape(bsz, s, -1) for a in (kn, vs, kr))
        xs = _ffn_ln(xs, ffn1_w_in[i].astype(BF16), ffn1_w_out[i].astype(BF16),
                     row(ln_g[i, 0]), row(ln_b[i, 0]), alpha)
        if i < n_a:
            w = fox_w_in[i]
            q, k, v, ft = _fox_proj(xs.reshape(bsz, s, d), w[:, :d].astype(BF16),
                                    w[:, d:2 * d].astype(BF16), w[:, 2 * d:3 * d].astype(BF16),
                                    w[:, 3 * d:].T.astype(BF16), LOG2E * fox_hd ** -0.5)
            negcum = _fox_decay(ft, fox_b_f[i].reshape(-1, 1))
            o = _fox_attn(q, k, v, negcum, fox_hd)
            w_o = fox_w_o[i]
        else:
            j = i - n_a
            wuq = mla_w_uq[j]
            q_rank = wuq.shape[0]
            wn = wuq[:, :, :nope].reshape(q_rank, -1).astype(BF16)
            wr3 = wuq[:, :, nope:]
            zpad = ((0, 0), (0, 0), (0, LANES - rope_dim))
            wr = jnp.pad(wr3, zpad).reshape(q_rank, -1).astype(BF16)
            wrr = jnp.pad(_rot_half_cols(wr3), zpad).reshape(q_rank, -1).astype(BF16)
            q = _mla_q(xs, mla_w_dq[j].astype(BF16), row(mla_q_norm[j]), wn, wr, wrr, c1, c2,
                       mla_heads, LOG2E * (nope + rope_dim) ** -0.5)
            o = _mla_attn(q.reshape(bsz, s, -1), kn, kr, vs, mla_heads)
            w_o = mla_w_o[j]
        xs = _proj_ln(o.reshape(m, -1), w_o.astype(BF16), xs, row(ln_g[i, 1]), row(ln_b[i, 1]), alpha)
        xs = _ffn_ln(xs, ffn2_w_in[i].astype(BF16), ffn2_w_out[i].astype(BF16),
                     row(ln_g[i, 2]), row(ln_b[i, 2]), alpha)
        xs = _ple_ln(xs, p2, i, ple_w_gate[i].astype(BF16), row(ple_b_gate[i]),
                     ple_w_proj[i].astype(BF16), row(ln_g[i, 3]), row(ln_b[i, 3]), alpha)
    return xs.reshape(bsz, s, d)
```

```python
import functools

import jax
import jax.numpy as jnp
from jax import lax
from jax.experimental import pallas as pl
from jax.experimental.pallas import tpu as pltpu

BF16 = jnp.bfloat16
F32 = jnp.float32

LN_EPS = 1e-5
RMS_EPS = 1e-6
ROPE_THETA = 10000.0

LANES = 128
VMEM_LIMIT_BYTES = 56 * 1024 * 1024
ROW_TILE = 512
ATTN_TILE = 512
FF_CHUNK = 256
SOFTMAX_ROWS = 16
MXU_TILE = 256
SUB_ROWS = 256
LOG2E = 1.4426950408889634

_NT = (((1,), (1,)), ((), ()))


def _params(n_grid):
    return pltpu.CompilerParams(dimension_semantics=("arbitrary",) * n_grid,
                                vmem_limit_bytes=VMEM_LIMIT_BYTES)


def _const_spec(shape):
    zeros = (0,) * len(shape)
    return pl.BlockSpec(shape, lambda *_: zeros, pipeline_mode=pl.Buffered(1))


def _row_spec(tm, cols):
    return pl.BlockSpec((tm, cols), lambda i: (i, 0))


def _layer_norm(z, g, b):
    mu = jnp.mean(z, axis=-1, keepdims=True)
    zc = z - mu
    var = jnp.mean(zc * zc, axis=-1, keepdims=True)
    return zc * lax.rsqrt(var + LN_EPS) * g + b


def _rms_norm(h, g):
    return h * lax.rsqrt(jnp.mean(h * h, axis=-1, keepdims=True) + RMS_EPS) * g


def _dot(a, b):
    return jnp.dot(a, b, preferred_element_type=F32)


def _ffn_ln_kernel(x_ref, win_ref, wout_ref, g_ref, b_ref, o_ref, act_ref, *, d_ff, alpha):
    x = x_ref[...]
    xb = x.astype(BF16)
    for c in range(d_ff // FF_CHUNK):
        lo = c * FF_CHUNK
        gate = _dot(xb, win_ref[:, lo:lo + FF_CHUNK])
        up = _dot(xb, win_ref[:, d_ff + lo:d_ff + lo + FF_CHUNK])
        act_ref[:, lo:lo + FF_CHUNK] = (gate * jax.nn.sigmoid(gate) * up).astype(BF16)
    y = _dot(act_ref[...], wout_ref[...])
    o_ref[...] = _layer_norm(alpha * x + 0.5 * y, g_ref[...], b_ref[...])


def _ffn_ln(x, w_in, w_out, g, b, alpha):
    m, d = x.shape
    d_ff = w_out.shape[0]
    assert d_ff % FF_CHUNK == 0 and m % ROW_TILE == 0
    tm = ROW_TILE
    return pl.pallas_call(
        functools.partial(_ffn_ln_kernel, d_ff=d_ff, alpha=alpha),
        grid=(m // tm,),
        in_specs=[_row_spec(tm, d), _const_spec(w_in.shape), _const_spec(w_out.shape),
                  _const_spec((1, d)), _const_spec((1, d))],
        out_specs=_row_spec(tm, d),
        out_shape=jax.ShapeDtypeStruct((m, d), F32),
        scratch_shapes=[pltpu.VMEM((tm, d_ff), BF16)],
        compiler_params=_params(1),
        name="ffn_ln",
    )(x, w_in, w_out, g, b)


def _proj_ln_kernel(a_ref, w_ref, x_ref, g_ref, b_ref, o_ref, *, alpha):
    for r in range(0, x_ref.shape[0], SUB_ROWS):
        rows = pl.ds(r, SUB_ROWS)
        y = _dot(a_ref[rows, :], w_ref[...])
        o_ref[rows, :] = _layer_norm(alpha * x_ref[rows, :] + y, g_ref[...], b_ref[...])


def _proj_ln(a, w, x, g, b, alpha):
    m, d = x.shape
    k = a.shape[1]
    tm = ROW_TILE
    return pl.pallas_call(
        functools.partial(_proj_ln_kernel, alpha=alpha),
        grid=(m // tm,),
        in_specs=[_row_spec(tm, k), _const_spec(w.shape), _row_spec(tm, d),
                  _const_spec((1, d)), _const_spec((1, d))],
        out_specs=_row_spec(tm, d),
        out_shape=jax.ShapeDtypeStruct((m, d), F32),
        compiler_params=_params(1),
        name="proj_ln",
    )(a, w, x, g, b)


def _ple_ln_kernel(x_ref, p_ref, wg_ref, bg_ref, wp_ref, g_ref, b_ref, o_ref, *, alpha):
    for r in range(0, x_ref.shape[0], SUB_ROWS):
        rows = pl.ds(r, SUB_ROWS)
        x = x_ref[rows, :]
        gate = jax.nn.sigmoid(_dot(x.astype(BF16), wg_ref[...]) + bg_ref[...])
        proj = _dot(p_ref[0, rows, :].astype(BF16), wp_ref[...])
        o_ref[rows, :] = _layer_norm(alpha * x + gate * proj, g_ref[...], b_ref[...])


def _ple_ln(x, p_all, layer, wg, bg, wp, g, b, alpha):
    m, d = x.shape
    pd = p_all.shape[-1]
    tm = ROW_TILE
    return pl.pallas_call(
        functools.partial(_ple_ln_kernel, alpha=alpha),
        grid=(m // tm,),
        in_specs=[_row_spec(tm, d), pl.BlockSpec((1, tm, pd), lambda i: (layer, i, 0)),
                  _const_spec(wg.shape), _const_spec((1, d)), _const_spec(wp.shape),
                  _const_spec((1, d)), _const_spec((1, d))],
        out_specs=_row_spec(tm, d),
        out_shape=jax.ShapeDtypeStruct((m, d), F32),
        compiler_params=_params(1),
        name="ple_ln",
    )(x, p_all, wg, bg, wp, g, b)


def _fox_proj_kernel(x_ref, wq_ref, wk_ref, wv_ref, wft_ref, q_ref, k_ref, v_ref, ft_ref, *, scale):
    for r in range(0, x_ref.shape[1], SUB_ROWS):
        rows = pl.ds(r, SUB_ROWS)
        xb = x_ref[0, rows, :].astype(BF16)
        q_ref[0, rows, :] = (_dot(xb, wq_ref[...]) * scale).astype(BF16)
        k_ref[0, rows, :] = _dot(xb, wk_ref[...]).astype(BF16)
        v_ref[0, rows, :] = _dot(xb, wv_ref[...]).astype(BF16)
        ft_ref[0, :, rows] = lax.dot_general(wft_ref[...], xb, _NT, preferred_element_type=F32)


def _fox_proj(x3, wq, wk, wv, wft, scale):
    bsz, s, d = x3.shape
    h = wft.shape[0]
    tm = ROW_TILE
    tok = pl.BlockSpec((1, tm, d), lambda b, i: (b, i, 0))
    return pl.pallas_call(
        functools.partial(_fox_proj_kernel, scale=scale),
        grid=(bsz, s // tm),
        in_specs=[tok, _const_spec(wq.shape), _const_spec(wk.shape), _const_spec(wv.shape),
                  _const_spec(wft.shape)],
        out_specs=[tok, tok, tok, pl.BlockSpec((1, h, tm), lambda b, i: (b, 0, i))],
        out_shape=[jax.ShapeDtypeStruct((bsz, s, d), BF16)] * 3
        + [jax.ShapeDtypeStruct((bsz, h, s), F32)],
        compiler_params=_params(2),
        name="fox_proj",
    )(x3, wq, wk, wv, wft)


def _fox_decay_kernel(f_ref, bf_ref, o_ref):
    f = f_ref[0] + bf_ref[...]
    c = jnp.minimum(f, 0.0) - jnp.log1p(jnp.exp(-jnp.abs(f)))
    lane = lax.broadcasted_iota(jnp.int32, c.shape, 1)
    shift = 1
    while shift < c.shape[1]:
        c = c + jnp.where(lane >= shift, pltpu.roll(c, shift, axis=1), 0.0)
        shift *= 2
    o_ref[0] = -LOG2E * c


def _fox_decay(ft, b_f):
    bsz, h, s = ft.shape
    blk = pl.BlockSpec((1, h, s), lambda b: (b, 0, 0))
    return pl.pallas_call(
        _fox_decay_kernel,
        grid=(bsz,),
        in_specs=[blk, _const_spec((h, 1))],
        out_specs=blk,
        out_shape=jax.ShapeDtypeStruct((bsz, h, s), F32),
        compiler_params=_params(1),
        name="fox_decay",
    )(ft, b_f)


def _kv_shared_kernel(x_ref, wd_ref, g_ref, wk_ref, wv_ref, c1_ref, c2_ref,
                      kn_ref, v_ref, kr_ref, *, rank):
    xb = x_ref[...].astype(BF16)
    h = _dot(xb, wd_ref[...])
    c = _rms_norm(h[:, :rank], g_ref[...]).astype(BF16)
    kn_ref[...] = _dot(c, wk_ref[...]).astype(BF16)
    v_ref[...] = _dot(c, wv_ref[...]).astype(BF16)
    hr = h[:, rank:]
    kr = hr * c1_ref[...] + pltpu.roll(hr, LANES // 2, axis=1) * c2_ref[...]
    kr_ref[...] = kr.astype(BF16)


def _kv_shared(x, wd, g, wk, wv, c1, c2, rank):
    m, d = x.shape
    s = c1.shape[0]
    tm = ROW_TILE
    nblk = s // tm
    tab = pl.BlockSpec((tm, LANES), lambda i: (i % nblk, 0))
    hv = wk.shape[1]
    return pl.pallas_call(
        functools.partial(_kv_shared_kernel, rank=rank),
        grid=(m // tm,),
        in_specs=[_row_spec(tm, d), _const_spec(wd.shape), _const_spec((1, rank)),
                  _const_spec(wk.shape), _const_spec(wv.shape), tab, tab],
        out_specs=[_row_spec(tm, hv), _row_spec(tm, hv), _row_spec(tm, LANES)],
        out_shape=[jax.ShapeDtypeStruct((m, hv), BF16), jax.ShapeDtypeStruct((m, hv), BF16),
                   jax.ShapeDtypeStruct((m, LANES), BF16)],
        compiler_params=_params(1),
        name="kv_shared",
    )(x, wd, g, wk, wv, c1, c2)


def _mla_q_kernel(x_ref, wdq_ref, g_ref, wn_ref, wr_ref, c1_ref, c2_ref, q_ref, *, heads, scale):
    for r in range(0, x_ref.shape[0], SUB_ROWS):
        rows = pl.ds(r, SUB_ROWS)
        xb = x_ref[rows, :].astype(BF16)
        cq = _rms_norm(_dot(xb, wdq_ref[...]), g_ref[...]).astype(BF16)
        qn = _dot(cq, wn_ref[...])
        qr = _dot(cq, wr_ref[...])
        c1 = c1_ref[rows, :]
        c2 = c2_ref[rows, :]
        for h in range(heads):
            lo = h * LANES
            q_ref[rows, 2 * lo:2 * lo + LANES] = (qn[:, lo:lo + LANES] * scale).astype(BF16)
            hr = qr[:, lo:lo + LANES]
            roped = hr * c1 + pltpu.roll(hr, LANES // 2, axis=1) * c2
            q_ref[rows, 2 * lo + LANES:2 * lo + 2 * LANES] = (roped * scale).astype(BF16)


def _mla_q(x, wdq, g, wn, wr, c1, c2, heads, scale):
    m, d = x.shape
    s = c1.shape[0]
    tm = ROW_TILE
    nblk = s // tm
    tab = pl.BlockSpec((tm, LANES), lambda i: (i % nblk, 0))
    qw = heads * 2 * LANES
    return pl.pallas_call(
        functools.partial(_mla_q_kernel, heads=heads, scale=scale),
        grid=(m // tm,),
        in_specs=[_row_spec(tm, d), _const_spec(wdq.shape), _const_spec((1, wdq.shape[1])),
                  _const_spec(wn.shape), _const_spec(wr.shape), tab, tab],
        out_specs=_row_spec(tm, qw),
        out_shape=jax.ShapeDtypeStruct((m, qw), BF16),
        compiler_params=_params(1),
        name="mla_q",
    )(x, wdq, g, wn, wr, c1, c2)


def _logits_tile(s_ref, q, k, diag, bias=None):
    t = s_ref.shape[0]
    for n0 in range(0, t, MXU_TILE):
        r0 = n0 if diag else 0
        s = lax.dot_general(q[r0:], k[n0:n0 + MXU_TILE], _NT, preferred_element_type=F32)
        if bias is not None:
            s = s + bias[:, n0:n0 + MXU_TILE]
        s_ref[r0:, n0:n0 + MXU_TILE] = s


def _softmax_rows(s_ref, p_ref, m_ref, l_ref, a_ref, idx, first, diag):
    t, tk = s_ref.shape
    for r in range(0, t, SOFTMAX_ROWS):
        rows = pl.ds(r, SOFTMAX_ROWS)
        live = min(tk, -(-(r + SOFTMAX_ROWS) // LANES) * LANES) if diag else tk
        s = s_ref[rows, :live]
        if diag:
            row = r + lax.broadcasted_iota(jnp.int32, s.shape, 0)
            col = lax.broadcasted_iota(jnp.int32, s.shape, 1)
            s = jnp.where(col <= row, s, -jnp.inf)
        m_cur = jnp.max(s, axis=1, keepdims=True)
        if first:
            m_next = jnp.broadcast_to(m_cur, (SOFTMAX_ROWS, LANES))
        else:
            m_prev = m_ref[idx, rows, :]
            m_next = jnp.maximum(m_prev, m_cur)
        p = jnp.exp2(s - jnp.concatenate([m_next] * (live // LANES), axis=1))
        l_cur = jnp.sum(p, axis=1, keepdims=True)
        if first:
            l_ref[idx, rows, :] = jnp.broadcast_to(l_cur, (SOFTMAX_ROWS, LANES))
        else:
            alpha = jnp.exp2(m_prev - m_next)
            l_ref[idx, rows, :] = alpha * l_ref[idx, rows, :] + l_cur
            a_ref[rows, :] = alpha
        m_ref[idx, rows, :] = m_next
        p_ref[rows, :live] = p.astype(BF16)
        read_end = (r // MXU_TILE + 1) * MXU_TILE if diag else tk
        if live < read_end:
            p_ref[rows, live:read_end] = jnp.zeros((SOFTMAX_ROWS, read_end - live), BF16)


def _values_tile(acc_ref, idx, a_ref, p_ref, v, first, diag):
    t = p_ref.shape[0]
    if not diag:
        pv = _dot(p_ref[...], v)
        acc_ref[idx] = pv if first else a_ref[...] * acc_ref[idx] + pv
        return
    blocks = [None] * (t // MXU_TILE)
    if not first:
        scaled = a_ref[...] * acc_ref[idx]
        blocks = [scaled[r:r + MXU_TILE] for r in range(0, t, MXU_TILE)]
    for n0 in range(0, t, MXU_TILE):
        pv = _dot(p_ref[n0:, n0:n0 + MXU_TILE], v[n0:n0 + MXU_TILE])
        for r in range(n0, t, MXU_TILE):
            part = pv[r - n0:r - n0 + MXU_TILE]
            b = r // MXU_TILE
            blocks[b] = part if blocks[b] is None else blocks[b] + part
    acc_ref[idx] = jnp.concatenate(blocks, axis=0)


def _causal_pipeline(n_tiles, n_heads, logits, softmax, values):
    last = n_tiles - 1
    for k in range(n_tiles + 2):
        for h in range(n_heads):
            if 0 <= k - 2:
                values(h, k - 2, k % 2, k - 2 == 0, k - 2 == last)
            if k < n_tiles:
                logits(h, k, k % 2, k == 0, k == last)
            if 0 <= k - 1 < n_tiles:
                softmax(h, k - 1, (k - 1) % 2, k - 1 == 0, k - 1 == last)


def _attn_scratch(t, n_heads):
    return ([pltpu.VMEM((2 * n_heads, t, t), F32), pltpu.VMEM((2 * n_heads, t, t), BF16),
             pltpu.VMEM((2 * n_heads, t, LANES), F32)] + [pltpu.VMEM((n_heads, t, LANES), F32)] * 3)


def _fox_attn_kernel(q_ref, k_ref, v_ref, nc_ref, o_ref, s_ref, p_ref, a_ref, m_ref, l_ref, acc_ref,
                     *, t, hd, n_q):
    qi = pl.program_id(2)
    lane = lax.broadcasted_iota(jnp.int32, (t, LANES), 1)

    def logits(h, j, slot, first, diag):
        q = q_ref[0]
        q_h = jnp.where((lane < hd) if h == 0 else (lane >= hd), q, jnp.zeros_like(q))
        _logits_tile(s_ref.at[2 * h + slot], q_h, k_ref[0, j * t:(j + 1) * t, :], diag,
                     bias=nc_ref[0, 0, h:h + 1, j * t:(j + 1) * t])

    def softmax(h, j, slot, first, diag):
        i = 2 * h + slot
        _softmax_rows(s_ref.at[i], p_ref.at[i], m_ref, l_ref, a_ref.at[i], h, first, diag)

    def values(h, j, slot, first, diag):
        i = 2 * h + slot
        _values_tile(acc_ref, h, a_ref.at[i], p_ref.at[i], v_ref[0, j * t:(j + 1) * t, :], first, diag)

    for c in range(n_q):
        @pl.when(qi == c)
        def _(c=c):
            _causal_pipeline(c + 1, 2, logits, softmax, values)
            o = jnp.where(lane < hd, acc_ref[0] / l_ref[0], acc_ref[1] / l_ref[1])
            o_ref[0] = o.astype(o_ref.dtype)


def _fox_attn(q, k, v, negcum, hd):
    bsz, s, d = q.shape
    t = ATTN_TILE
    assert 2 * hd == LANES and s % t == 0
    pairs = d // LANES
    nblk = s // t
    qspec = pl.BlockSpec((1, t, LANES), lambda b, p, i: (b, i, p))
    kvspec = pl.BlockSpec((1, s, LANES), lambda b, p, i: (b, 0, p))
    return pl.pallas_call(
        functools.partial(_fox_attn_kernel, t=t, hd=hd, n_q=nblk),
        grid=(bsz, pairs, nblk),
        in_specs=[qspec, kvspec, kvspec,
                  pl.BlockSpec((1, 1, 2, s), lambda b, p, i: (b, p, 0, 0))],
        out_specs=qspec,
        out_shape=jax.ShapeDtypeStruct((bsz, s, d), BF16),
        scratch_shapes=_attn_scratch(t, 2),
        compiler_params=_params(3),
        name="fox_attn",
    )(q, k, v, negcum.reshape(bsz, pairs, 2, s))


def _mla_attn_kernel(q_ref, kn_ref, kr_ref, v_ref, o_ref, kf_ref, s_ref, p_ref, a_ref,
                     m_ref, l_ref, acc_ref, *, t, n_q):
    qi = pl.program_id(2)

    @pl.when(qi == 0)
    def _():
        kf_ref[:, :LANES] = kn_ref[0]
        kf_ref[:, LANES:] = kr_ref[0]

    def logits(h, j, slot, first, diag):
        _logits_tile(s_ref.at[slot], q_ref[0], kf_ref[j * t:(j + 1) * t, :], diag)

    def softmax(h, j, slot, first, diag):
        _softmax_rows(s_ref.at[slot], p_ref.at[slot], m_ref, l_ref, a_ref.at[slot], 0, first, diag)

    def values(h, j, slot, first, diag):
        _values_tile(acc_ref, 0, a_ref.at[slot], p_ref.at[slot], v_ref[0, j * t:(j + 1) * t, :],
                     first, diag)

    for c in range(n_q):
        @pl.when(qi == c)
        def _(c=c):
            _causal_pipeline(c + 1, 1, logits, softmax, values)
            o_ref[0] = (acc_ref[0] / l_ref[0]).astype(o_ref.dtype)


def _mla_attn(q, kn, kr, v, heads):
    bsz, s, _ = q.shape
    t = ATTN_TILE
    assert s % t == 0
    nblk = s // t
    return pl.pallas_call(
        functools.partial(_mla_attn_kernel, t=t, n_q=nblk),
        grid=(bsz, heads, nblk),
        in_specs=[pl.BlockSpec((1, t, 2 * LANES), lambda b, h, i: (b, i, h)),
                  pl.BlockSpec((1, s, LANES), lambda b, h, i: (b, 0, h)),
                  pl.BlockSpec((1, s, LANES), lambda b, h, i: (b, 0, 0)),
                  pl.BlockSpec((1, s, LANES), lambda b, h, i: (b, 0, h))],
        out_specs=pl.BlockSpec((1, t, LANES), lambda b, h, i: (b, i, h)),
        out_shape=jax.ShapeDtypeStruct((bsz, s, heads * LANES), BF16),
        scratch_shapes=[pltpu.VMEM((s, 2 * LANES), BF16)] + _attn_scratch(t, 1),
        compiler_params=_params(3),
        name="mla_attn",
    )(q, kn, kr, v)


def _rot_half_cols(w):
    half = w.shape[-1] // 2
    return jnp.concatenate([-w[..., half:], w[..., :half]], axis=-1)


def kernel(x, p, ffn1_w_in, ffn1_w_out, ffn2_w_in, ffn2_w_out, ln_g, ln_b, ple_w_gate, ple_b_gate, ple_w_proj, fox_w_in, fox_b_f, fox_w_o, mla_w_dq, mla_q_norm, mla_w_uq, mla_w_o, kv_w_down, kv_norm, kv_w_up):
    bsz, s, d = x.shape
    depth = ffn1_w_in.shape[0]
    n_a = fox_w_in.shape[0]
    m = bsz * s
    alpha = (2 * depth) ** 0.25
    fox_heads = fox_b_f.shape[1]
    fox_hd = d // fox_heads
    rank, mla_heads = kv_w_up.shape[0], kv_w_up.shape[1]
    rope_dim = kv_w_down.shape[1] - rank
    nope = mla_w_uq.shape[3] - rope_dim
    assert nope == LANES and kv_w_up.shape[2] == 2 * LANES and 2 * rope_dim == LANES

    half = rope_dim // 2
    inv = ROPE_THETA ** (-jnp.arange(half, dtype=F32) * (2.0 / rope_dim))
    ang = jnp.arange(s, dtype=F32)[:, None] * inv[None, :]
    pad = jnp.zeros((s, LANES - rope_dim), F32)
    c1 = jnp.concatenate([jnp.cos(ang), jnp.cos(ang), pad], axis=1)
    c2 = jnp.concatenate([jnp.sin(ang), jnp.sin(ang), pad], axis=1)

    p2 = p.reshape(depth, m, p.shape[-1])
    row = lambda a: a.reshape(1, -1)
    xs = x.reshape(m, d)
    kn = kr = vs = None
    for i in range(depth):
        if i == n_a:
            wdr = kv_w_down[:, rank:]
            wd = jnp.concatenate([kv_w_down[:, :rank], wdr, _rot_half_cols(wdr)], axis=1).astype(BF16)
            wk = kv_w_up[:, :, :nope].reshape(rank, -1).astype(BF16)
            wv = kv_w_up[:, :, nope:].reshape(rank, -1).astype(BF16)
            kn, vs, kr = _kv_shared(xs, wd, row(kv_norm), wk, wv, c1, c2, rank)
            kn, vs, kr = (a.reshape(bsz, s, -1) for a in (kn, vs, kr))
        xs = _ffn_ln(xs, ffn1_w_in[i].astype(BF16), ffn1_w_out[i].astype(BF16),
                     row(ln_g[i, 0]), row(ln_b[i, 0]), alpha)
        if i < n_a:
            w = fox_w_in[i]
            q, k, v, ft = _fox_proj(xs.reshape(bsz, s, d), w[:, :d].astype(BF16),
                                    w[:, d:2 * d].astype(BF16), w[:, 2 * d:3 * d].astype(BF16),
                                    w[:, 3 * d:].T.astype(BF16), LOG2E * fox_hd ** -0.5)
            negcum = _fox_decay(ft, fox_b_f[i].reshape(-1, 1))
            o = _fox_attn(q, k, v, negcum, fox_hd)
            w_o = fox_w_o[i]
        else:
            j = i - n_a
            wuq = mla_w_uq[j]
            q_rank = wuq.shape[0]
            wn = wuq[:, :, :nope].reshape(q_rank, -1).astype(BF16)
            wr3 = wuq[:, :, nope:]
            wr = jnp.concatenate([wr3, _rot_half_cols(wr3)], axis=-1).reshape(q_rank, -1).astype(BF16)
            q = _mla_q(xs, mla_w_dq[j].astype(BF16), row(mla_q_norm[j]), wn, wr, c1, c2,
                       mla_heads, LOG2E * (nope + rope_dim) ** -0.5)
            o = _mla_attn(q.reshape(bsz, s, -1), kn, kr, vs, mla_heads)
            w_o = mla_w_o[j]
        xs = _proj_ln(o.reshape(m, -1), w_o.astype(BF16), xs, row(ln_g[i, 1]), row(ln_b[i, 1]), alpha)
        xs = _ffn_ln(xs, ffn2_w_in[i].astype(BF16), ffn2_w_out[i].astype(BF16),
                     row(ln_g[i, 2]), row(ln_b[i, 2]), alpha)
        xs = _ple_ln(xs, p2, i, ple_w_gate[i].astype(BF16), row(ple_b_gate[i]),
                     ple_w_proj[i].astype(BF16), row(ln_g[i, 3]), row(ln_b[i, 3]), alpha)
    return xs.reshape(bsz, s, d)
```

```python
import functools

import jax
import jax.numpy as jnp
from jax import lax
from jax.experimental import pallas as pl
from jax.experimental.pallas import tpu as pltpu

BF16 = jnp.bfloat16
F32 = jnp.float32

LN_EPS = 1e-5
RMS_EPS = 1e-6
ROPE_THETA = 10000.0

LANES = 128
VMEM_LIMIT_BYTES = 56 * 1024 * 1024
ROW_TILE = 512
ATTN_TILE = 512
FF_CHUNK = 256
SOFTMAX_ROWS = 16
MXU_TILE = 256
SUB_ROWS = 256
LOG2E = 1.4426950408889634

_NT = (((1,), (1,)), ((), ()))


def _params(n_grid):
    return pltpu.CompilerParams(dimension_semantics=("arbitrary",) * n_grid,
                                vmem_limit_bytes=VMEM_LIMIT_BYTES)


def _const_spec(shape):
    zeros = (0,) * len(shape)
    return pl.BlockSpec(shape, lambda *_: zeros, pipeline_mode=pl.Buffered(1))


def _row_spec(tm, cols):
    return pl.BlockSpec((tm, cols), lambda i: (i, 0))


def _layer_norm(z, g, b):
    mu = jnp.mean(z, axis=-1, keepdims=True)
    zc = z - mu
    var = jnp.mean(zc * zc, axis=-1, keepdims=True)
    return zc * lax.rsqrt(var + LN_EPS) * g + b


def _rms_norm(h, g):
    return h * lax.rsqrt(jnp.mean(h * h, axis=-1, keepdims=True) + RMS_EPS) * g


def _dot(a, b):
    return jnp.dot(a, b, preferred_element_type=F32)


def _ffn_ln_kernel(x_ref, win_ref, wout_ref, g_ref, b_ref, o_ref, act_ref, *, d_ff, alpha):
    x = x_ref[...]
    xb = x.astype(BF16)
    for c in range(d_ff // FF_CHUNK):
        lo = c * FF_CHUNK
        gate = _dot(xb, win_ref[:, lo:lo + FF_CHUNK])
        up = _dot(xb, win_ref[:, d_ff + lo:d_ff + lo + FF_CHUNK])
        act_ref[:, lo:lo + FF_CHUNK] = (gate * jax.nn.sigmoid(gate) * up).astype(BF16)
    y = _dot(act_ref[...], wout_ref[...])
    o_ref[...] = _layer_norm(alpha * x + 0.5 * y, g_ref[...], b_ref[...])


def _ffn_ln(x, w_in, w_out, g, b, alpha):
    m, d = x.shape
    d_ff = w_out.shape[0]
    assert d_ff % FF_CHUNK == 0 and m % ROW_TILE == 0
    tm = ROW_TILE
    return pl.pallas_call(
        functools.partial(_ffn_ln_kernel, d_ff=d_ff, alpha=alpha),
        grid=(m // tm,),
        in_specs=[_row_spec(tm, d), _const_spec(w_in.shape), _const_spec(w_out.shape),
                  _const_spec((1, d)), _const_spec((1, d))],
        out_specs=_row_spec(tm, d),
        out_shape=jax.ShapeDtypeStruct((m, d), F32),
        scratch_shapes=[pltpu.VMEM((tm, d_ff), BF16)],
        compiler_params=_params(1),
        name="ffn_ln",
    )(x, w_in, w_out, g, b)


def _post_mix_kernel(a_ref, x_ref, p_ref, wo_ref, win_ref, wout_ref, wg_ref, bg_ref, wp_ref,
                     g_ref, b_ref, o_ref, act_ref, xs_ref, *, d_ff, alpha):
    tm = x_ref.shape[0]
    for r in range(0, tm, SUB_ROWS):
        rows = pl.ds(r, SUB_ROWS)
        y = _dot(a_ref[rows, :], wo_ref[...])
        xs_ref[rows, :] = _layer_norm(alpha * x_ref[rows, :] + y, g_ref[1:2, :], b_ref[1:2, :])
    x1 = xs_ref[...]
    xb = x1.astype(BF16)
    for c in range(d_ff // FF_CHUNK):
        lo = c * FF_CHUNK
        gate = _dot(xb, win_ref[:, lo:lo + FF_CHUNK])
        up = _dot(xb, win_ref[:, d_ff + lo:d_ff + lo + FF_CHUNK])
        act_ref[:, lo:lo + FF_CHUNK] = (gate * jax.nn.sigmoid(gate) * up).astype(BF16)
    y = _dot(act_ref[...], wout_ref[...])
    xs_ref[...] = _layer_norm(alpha * x1 + 0.5 * y, g_ref[2:3, :], b_ref[2:3, :])
    for r in range(0, tm, SUB_ROWS):
        rows = pl.ds(r, SUB_ROWS)
        x2 = xs_ref[rows, :]
        gate = jax.nn.sigmoid(_dot(x2.astype(BF16), wg_ref[...]) + bg_ref[...])
        proj = _dot(p_ref[0, rows, :].astype(BF16), wp_ref[...])
        o_ref[rows, :] = _layer_norm(alpha * x2 + gate * proj, g_ref[3:4, :], b_ref[3:4, :])


def _post_mix(a, x, p_all, layer, wo, w_in, w_out, wg, bg, wp, g4, b4, alpha):
    m, d = x.shape
    k = a.shape[1]
    pd = p_all.shape[-1]
    d_ff = w_out.shape[0]
    assert d_ff % FF_CHUNK == 0 and m % ROW_TILE == 0
    tm = ROW_TILE
    return pl.pallas_call(
        functools.partial(_post_mix_kernel, d_ff=d_ff, alpha=alpha),
        grid=(m // tm,),
        in_specs=[_row_spec(tm, k), _row_spec(tm, d),
                  pl.BlockSpec((1, tm, pd), lambda i: (layer, i, 0)),
                  _const_spec(wo.shape), _const_spec(w_in.shape), _const_spec(w_out.shape),
                  _const_spec(wg.shape), _const_spec((1, d)), _const_spec(wp.shape),
                  _const_spec(g4.shape), _const_spec(b4.shape)],
        out_specs=_row_spec(tm, d),
        out_shape=jax.ShapeDtypeStruct((m, d), F32),
        scratch_shapes=[pltpu.VMEM((tm, d_ff), BF16), pltpu.VMEM((tm, d), F32)],
        compiler_params=_params(1),
        name="post_mix",
    )(a, x, p_all, wo, w_in, w_out, wg, bg, wp, g4, b4)


def _fox_proj_kernel(x_ref, wq_ref, wk_ref, wv_ref, wft_ref, q_ref, k_ref, v_ref, ft_ref, *, scale):
    for r in range(0, x_ref.shape[1], SUB_ROWS):
        rows = pl.ds(r, SUB_ROWS)
        xb = x_ref[0, rows, :].astype(BF16)
        q_ref[0, rows, :] = (_dot(xb, wq_ref[...]) * scale).astype(BF16)
        k_ref[0, rows, :] = _dot(xb, wk_ref[...]).astype(BF16)
        v_ref[0, rows, :] = _dot(xb, wv_ref[...]).astype(BF16)
        ft_ref[0, :, rows] = lax.dot_general(wft_ref[...], xb, _NT, preferred_element_type=F32)


def _fox_proj(x3, wq, wk, wv, wft, scale):
    bsz, s, d = x3.shape
    h = wft.shape[0]
    tm = ROW_TILE
    tok = pl.BlockSpec((1, tm, d), lambda b, i: (b, i, 0))
    return pl.pallas_call(
        functools.partial(_fox_proj_kernel, scale=scale),
        grid=(bsz, s // tm),
        in_specs=[tok, _const_spec(wq.shape), _const_spec(wk.shape), _const_spec(wv.shape),
                  _const_spec(wft.shape)],
        out_specs=[tok, tok, tok, pl.BlockSpec((1, h, tm), lambda b, i: (b, 0, i))],
        out_shape=[jax.ShapeDtypeStruct((bsz, s, d), BF16)] * 3
        + [jax.ShapeDtypeStruct((bsz, h, s), F32)],
        compiler_params=_params(2),
        name="fox_proj",
    )(x3, wq, wk, wv, wft)


def _fox_decay_kernel(f_ref, bf_ref, o_ref):
    f = f_ref[0] + bf_ref[...]
    c = jnp.minimum(f, 0.0) - jnp.log1p(jnp.exp(-jnp.abs(f)))
    lane = lax.broadcasted_iota(jnp.int32, c.shape, 1)
    shift = 1
    while shift < c.shape[1]:
        c = c + jnp.where(lane >= shift, pltpu.roll(c, shift, axis=1), 0.0)
        shift *= 2
    o_ref[0] = -LOG2E * c


def _fox_decay(ft, b_f):
    bsz, h, s = ft.shape
    blk = pl.BlockSpec((1, h, s), lambda b: (b, 0, 0))
    return pl.pallas_call(
        _fox_decay_kernel,
        grid=(bsz,),
        in_specs=[blk, _const_spec((h, 1))],
        out_specs=blk,
        out_shape=jax.ShapeDtypeStruct((bsz, h, s), F32),
        compiler_params=_params(1),
        name="fox_decay",
    )(ft, b_f)


def _kv_shared_kernel(x_ref, wd_ref, g_ref, wk_ref, wv_ref, c1_ref, c2_ref,
                      kn_ref, v_ref, kr_ref, *, rank):
    xb = x_ref[...].astype(BF16)
    h = _dot(xb, wd_ref[...])
    c = _rms_norm(h[:, :rank], g_ref[...]).astype(BF16)
    kn_ref[...] = _dot(c, wk_ref[...]).astype(BF16)
    v_ref[...] = _dot(c, wv_ref[...]).astype(BF16)
    hr = h[:, rank:]
    kr = hr * c1_ref[...] + pltpu.roll(hr, LANES // 2, axis=1) * c2_ref[...]
    kr_ref[...] = kr.astype(BF16)


def _kv_shared(x, wd, g, wk, wv, c1, c2, rank):
    m, d = x.shape
    s = c1.shape[0]
    tm = ROW_TILE
    nblk = s // tm
    tab = pl.BlockSpec((tm, LANES), lambda i: (i % nblk, 0))
    hv = wk.shape[1]
    return pl.pallas_call(
        functools.partial(_kv_shared_kernel, rank=rank),
        grid=(m // tm,),
        in_specs=[_row_spec(tm, d), _const_spec(wd.shape), _const_spec((1, rank)),
                  _const_spec(wk.shape), _const_spec(wv.shape), tab, tab],
        out_specs=[_row_spec(tm, hv), _row_spec(tm, hv), _row_spec(tm, LANES)],
        out_shape=[jax.ShapeDtypeStruct((m, hv), BF16), jax.ShapeDtypeStruct((m, hv), BF16),
                   jax.ShapeDtypeStruct((m, LANES), BF16)],
        compiler_params=_params(1),
        name="kv_shared",
    )(x, wd, g, wk, wv, c1, c2)


def _mla_q_kernel(x_ref, wdq_ref, g_ref, wn_ref, wr_ref, c1_ref, c2_ref, q_ref, *, heads, scale):
    for r in range(0, x_ref.shape[0], SUB_ROWS):
        rows = pl.ds(r, SUB_ROWS)
        xb = x_ref[rows, :].astype(BF16)
        cq = _rms_norm(_dot(xb, wdq_ref[...]), g_ref[...]).astype(BF16)
        qn = _dot(cq, wn_ref[...])
        qr = _dot(cq, wr_ref[...])
        c1 = c1_ref[rows, :]
        c2 = c2_ref[rows, :]
        for h in range(heads):
            lo = h * LANES
            q_ref[rows, 2 * lo:2 * lo + LANES] = (qn[:, lo:lo + LANES] * scale).astype(BF16)
            hr = qr[:, lo:lo + LANES]
            roped = hr * c1 + pltpu.roll(hr, LANES // 2, axis=1) * c2
            q_ref[rows, 2 * lo + LANES:2 * lo + 2 * LANES] = (roped * scale).astype(BF16)


def _mla_q(x, wdq, g, wn, wr, c1, c2, heads, scale):
    m, d = x.shape
    s = c1.shape[0]
    tm = ROW_TILE
    nblk = s // tm
    tab = pl.BlockSpec((tm, LANES), lambda i: (i % nblk, 0))
    qw = heads * 2 * LANES
    return pl.pallas_call(
        functools.partial(_mla_q_kernel, heads=heads, scale=scale),
        grid=(m // tm,),
        in_specs=[_row_spec(tm, d), _const_spec(wdq.shape), _const_spec((1, wdq.shape[1])),
                  _const_spec(wn.shape), _const_spec(wr.shape), tab, tab],
        out_specs=_row_spec(tm, qw),
        out_shape=jax.ShapeDtypeStruct((m, qw), BF16),
        compiler_params=_params(1),
        name="mla_q",
    )(x, wdq, g, wn, wr, c1, c2)


def _logits_tile(s_ref, q, k, diag, bias=None):
    t = s_ref.shape[0]
    for n0 in range(0, t, MXU_TILE):
        r0 = n0 if diag else 0
        s = lax.dot_general(q[r0:], k[n0:n0 + MXU_TILE], _NT, preferred_element_type=F32)
        if bias is not None:
            s = s + bias[:, n0:n0 + MXU_TILE]
        s_ref[r0:, n0:n0 + MXU_TILE] = s


def _softmax_rows(s_ref, p_ref, m_ref, l_ref, a_ref, idx, first, diag):
    t, tk = s_ref.shape
    for r in range(0, t, SOFTMAX_ROWS):
        rows = pl.ds(r, SOFTMAX_ROWS)
        live = min(tk, -(-(r + SOFTMAX_ROWS) // LANES) * LANES) if diag else tk
        s = s_ref[rows, :live]
        if diag:
            row = r + lax.broadcasted_iota(jnp.int32, s.shape, 0)
            col = lax.broadcasted_iota(jnp.int32, s.shape, 1)
            s = jnp.where(col <= row, s, -jnp.inf)
        m_cur = jnp.max(s, axis=1, keepdims=True)
        if first:
            m_next = jnp.broadcast_to(m_cur, (SOFTMAX_ROWS, LANES))
        else:
            m_prev = m_ref[idx, rows, :]
            m_next = jnp.maximum(m_prev, m_cur)
        p = jnp.exp2(s - jnp.concatenate([m_next] * (live // LANES), axis=1))
        l_cur = jnp.sum(p, axis=1, keepdims=True)
        if first:
            l_ref[idx, rows, :] = jnp.broadcast_to(l_cur, (SOFTMAX_ROWS, LANES))
        else:
            alpha = jnp.exp2(m_prev - m_next)
            l_ref[idx, rows, :] = alpha * l_ref[idx, rows, :] + l_cur
            a_ref[rows, :] = alpha
        m_ref[idx, rows, :] = m_next
        p_ref[rows, :live] = p.astype(BF16)
        read_end = (r // MXU_TILE + 1) * MXU_TILE if diag else tk
        if live < read_end:
            p_ref[rows, live:read_end] = jnp.zeros((SOFTMAX_ROWS, read_end - live), BF16)


def _values_tile(acc_ref, idx, a_ref, p_ref, v, first, diag):
    t = p_ref.shape[0]
    if not diag:
        pv = _dot(p_ref[...], v)
        acc_ref[idx] = pv if first else a_ref[...] * acc_ref[idx] + pv
        return
    blocks = [None] * (t // MXU_TILE)
    if not first:
        scaled = a_ref[...] * acc_ref[idx]
        blocks = [scaled[r:r + MXU_TILE] for r in range(0, t, MXU_TILE)]
    for n0 in range(0, t, MXU_TILE):
        pv = _dot(p_ref[n0:, n0:n0 + MXU_TILE], v[n0:n0 + MXU_TILE])
        for r in range(n0, t, MXU_TILE):
            part = pv[r - n0:r - n0 + MXU_TILE]
            b = r // MXU_TILE
            blocks[b] = part if blocks[b] is None else blocks[b] + part
    acc_ref[idx] = jnp.concatenate(blocks, axis=0)


def _causal_pipeline(n_tiles, n_heads, logits, softmax, values):
    last = n_tiles - 1
    for k in range(n_tiles + 2):
        for h in range(n_heads):
            if 0 <= k - 2:
                values(h, k - 2, k % 2, k - 2 == 0, k - 2 == last)
            if k < n_tiles:
                logits(h, k, k % 2, k == 0, k == last)
            if 0 <= k - 1 < n_tiles:
                softmax(h, k - 1, (k - 1) % 2, k - 1 == 0, k - 1 == last)


def _attn_scratch(t, n_heads):
    return ([pltpu.VMEM((2 * n_heads, t, t), F32), pltpu.VMEM((2 * n_heads, t, t), BF16),
             pltpu.VMEM((2 * n_heads, t, LANES), F32)] + [pltpu.VMEM((n_heads, t, LANES), F32)] * 3)


def _fox_attn_kernel(q_ref, k_ref, v_ref, nc_ref, o_ref, s_ref, p_ref, a_ref, m_ref, l_ref, acc_ref,
                     *, t, hd, n_q):
    qi = pl.program_id(2)
    lane = lax.broadcasted_iota(jnp.int32, (t, LANES), 1)

    def logits(h, j, slot, first, diag):
        q = q_ref[0]
        q_h = jnp.where((lane < hd) if h == 0 else (lane >= hd), q, jnp.zeros_like(q))
        _logits_tile(s_ref.at[2 * h + slot], q_h, k_ref[0, j * t:(j + 1) * t, :], diag,
                     bias=nc_ref[0, 0, h:h + 1, j * t:(j + 1) * t])

    def softmax(h, j, slot, first, diag):
        i = 2 * h + slot
        _softmax_rows(s_ref.at[i], p_ref.at[i], m_ref, l_ref, a_ref.at[i], h, first, diag)

    def values(h, j, slot, first, diag):
        i = 2 * h + slot
        _values_tile(acc_ref, h, a_ref.at[i], p_ref.at[i], v_ref[0, j * t:(j + 1) * t, :], first, diag)

    for c in range(n_q):
        @pl.when(qi == c)
        def _(c=c):
            _causal_pipeline(c + 1, 2, logits, softmax, values)
            o = jnp.where(lane < hd, acc_ref[0] / l_ref[0], acc_ref[1] / l_ref[1])
            o_ref[0] = o.astype(o_ref.dtype)


def _fox_attn(q, k, v, negcum, hd):
    bsz, s, d = q.shape
    t = ATTN_TILE
    assert 2 * hd == LANES and s % t == 0
    pairs = d // LANES
    nblk = s // t
    qspec = pl.BlockSpec((1, t, LANES), lambda b, p, i: (b, i, p))
    kvspec = pl.BlockSpec((1, s, LANES), lambda b, p, i: (b, 0, p))
    return pl.pallas_call(
        functools.partial(_fox_attn_kernel, t=t, hd=hd, n_q=nblk),
        grid=(bsz, pairs, nblk),
        in_specs=[qspec, kvspec, kvspec,
                  pl.BlockSpec((1, 1, 2, s), lambda b, p, i: (b, p, 0, 0))],
        out_specs=qspec,
        out_shape=jax.ShapeDtypeStruct((bsz, s, d), BF16),
        scratch_shapes=_attn_scratch(t, 2),
        compiler_params=_params(3),
        name="fox_attn",
    )(q, k, v, negcum.reshape(bsz, pairs, 2, s))


def _mla_attn_kernel(q_ref, kn_ref, kr_ref, v_ref, o_ref, kf_ref, s_ref, p_ref, a_ref,
                     m_ref, l_ref, acc_ref, *, t, n_q):
    qi = pl.program_id(2)

    @pl.when(qi == 0)
    def _():
        kf_ref[:, :LANES] = kn_ref[0]
        kf_ref[:, LANES:] = kr_ref[0]

    def logits(h, j, slot, first, diag):
        _logits_tile(s_ref.at[slot], q_ref[0], kf_ref[j * t:(j + 1) * t, :], diag)

    def softmax(h, j, slot, first, diag):
        _softmax_rows(s_ref.at[slot], p_ref.at[slot], m_ref, l_ref, a_ref.at[slot], 0, first, diag)

    def values(h, j, slot, first, diag):
        _values_tile(acc_ref, 0, a_ref.at[slot], p_ref.at[slot], v_ref[0, j * t:(j + 1) * t, :],
                     first, diag)

    for c in range(n_q):
        @pl.when(qi == c)
        def _(c=c):
            _causal_pipeline(c + 1, 1, logits, softmax, values)
            o_ref[0] = (acc_ref[0] / l_ref[0]).astype(o_ref.dtype)


def _mla_attn(q, kn, kr, v, heads):
    bsz, s, _ = q.shape
    t = ATTN_TILE
    assert s % t == 0
    nblk = s // t
    return pl.pallas_call(
        functools.partial(_mla_attn_kernel, t=t, n_q=nblk),
        grid=(bsz, heads, nblk),
        in_specs=[pl.BlockSpec((1, t, 2 * LANES), lambda b, h, i: (b, i, h)),
                  pl.BlockSpec((1, s, LANES), lambda b, h, i: (b, 0, h)),
                  pl.BlockSpec((1, s, LANES), lambda b, h, i: (b, 0, 0)),
                  pl.BlockSpec((1, s, LANES), lambda b, h, i: (b, 0, h))],
        out_specs=pl.BlockSpec((1, t, LANES), lambda b, h, i: (b, i, h)),
        out_shape=jax.ShapeDtypeStruct((bsz, s, heads * LANES), BF16),
        scratch_shapes=[pltpu.VMEM((s, 2 * LANES), BF16)] + _attn_scratch(t, 1),
        compiler_params=_params(3),
        name="mla_attn",
    )(q, kn, kr, v)


def _rot_half_cols(w):
    half = w.shape[-1] // 2
    return jnp.concatenate([-w[..., half:], w[..., :half]], axis=-1)


def kernel(x, p, ffn1_w_in, ffn1_w_out, ffn2_w_in, ffn2_w_out, ln_g, ln_b, ple_w_gate, ple_b_gate, ple_w_proj, fox_w_in, fox_b_f, fox_w_o, mla_w_dq, mla_q_norm, mla_w_uq, mla_w_o, kv_w_down, kv_norm, kv_w_up):
    bsz, s, d = x.shape
    depth = ffn1_w_in.shape[0]
    n_a = fox_w_in.shape[0]
    m = bsz * s
    alpha = (2 * depth) ** 0.25
    fox_heads = fox_b_f.shape[1]
    fox_hd = d // fox_heads
    rank, mla_heads = kv_w_up.shape[0], kv_w_up.shape[1]
    rope_dim = kv_w_down.shape[1] - rank
    nope = mla_w_uq.shape[3] - rope_dim
    assert nope == LANES and kv_w_up.shape[2] == 2 * LANES and 2 * rope_dim == LANES

    half = rope_dim // 2
    inv = ROPE_THETA ** (-jnp.arange(half, dtype=F32) * (2.0 / rope_dim))
    ang = jnp.arange(s, dtype=F32)[:, None] * inv[None, :]
    pad = jnp.zeros((s, LANES - rope_dim), F32)
    c1 = jnp.concatenate([jnp.cos(ang), jnp.cos(ang), pad], axis=1)
    c2 = jnp.concatenate([jnp.sin(ang), jnp.sin(ang), pad], axis=1)

    p2 = p.reshape(depth, m, p.shape[-1])
    row = lambda a: a.reshape(1, -1)
    xs = x.reshape(m, d)
    kn = kr = vs = None
    for i in range(depth):
        if i == n_a:
            wdr = kv_w_down[:, rank:]
            wd = jnp.concatenate([kv_w_down[:, :rank], wdr, _rot_half_cols(wdr)], axis=1).astype(BF16)
            wk = kv_w_up[:, :, :nope].reshape(rank, -1).astype(BF16)
            wv = kv_w_up[:, :, nope:].reshape(rank, -1).astype(BF16)
            kn, vs, kr = _kv_shared(xs, wd, row(kv_norm), wk, wv, c1, c2, rank)
            kn, vs, kr = (a.reshape(bsz, s, -1) for a in (kn, vs, kr))
        xs = _ffn_ln(xs, ffn1_w_in[i].astype(BF16), ffn1_w_out[i].astype(BF16),
                     row(ln_g[i, 0]), row(ln_b[i, 0]), alpha)
        if i < n_a:
            w = fox_w_in[i]
            q, k, v, ft = _fox_proj(xs.reshape(bsz, s, d), w[:, :d].astype(BF16),
                                    w[:, d:2 * d].astype(BF16), w[:, 2 * d:3 * d].astype(BF16),
                                    w[:, 3 * d:].T.astype(BF16), LOG2E * fox_hd ** -0.5)
            negcum = _fox_decay(ft, fox_b_f[i].reshape(-1, 1))
            o = _fox_attn(q, k, v, negcum, fox_hd)
            w_o = fox_w_o[i]
        else:
            j = i - n_a
            wuq = mla_w_uq[j]
            q_rank = wuq.shape[0]
            wn = wuq[:, :, :nope].reshape(q_rank, -1).astype(BF16)
            wr3 = wuq[:, :, nope:]
            wr = jnp.concatenate([wr3, _rot_half_cols(wr3)], axis=-1).reshape(q_rank, -1).astype(BF16)
            q = _mla_q(xs, mla_w_dq[j].astype(BF16), row(mla_q_norm[j]), wn, wr, c1, c2,
                       mla_heads, LOG2E * (nope + rope_dim) ** -0.5)
            o = _mla_attn(q.reshape(bsz, s, -1), kn, kr, vs, mla_heads)
            w_o = mla_w_o[j]
        xs = _post_mix(o.reshape(m, -1), xs, p2, i, w_o.astype(BF16),
                       ffn2_w_in[i].astype(BF16), ffn2_w_out[i].astype(BF16),
                       ple_w_gate[i].astype(BF16), row(ple_b_gate[i]), ple_w_proj[i].astype(BF16),
                       ln_g[i], ln_b[i], alpha)
    return xs.reshape(bsz, s, d)
```

```python
import functools

import jax
import jax.numpy as jnp
from jax import lax
from jax.experimental import pallas as pl
from jax.experimental.pallas import tpu as pltpu

BF16 = jnp.bfloat16
F32 = jnp.float32

LN_EPS = 1e-5
RMS_EPS = 1e-6
ROPE_THETA = 10000.0

LANES = 128
VMEM_LIMIT_BYTES = 56 * 1024 * 1024
ROW_TILE = 512
ATTN_TILE = 512
FF_CHUNK = 256
SOFTMAX_ROWS = 16
MXU_TILE = 256
SUB_ROWS = 256
LOG2E = 1.4426950408889634

_NT = (((1,), (1,)), ((), ()))


def _params(n_grid):
    return pltpu.CompilerParams(dimension_semantics=("arbitrary",) * n_grid,
                                vmem_limit_bytes=VMEM_LIMIT_BYTES)


def _const_spec(shape):
    zeros = (0,) * len(shape)
    return pl.BlockSpec(shape, lambda *_: zeros, pipeline_mode=pl.Buffered(1))


def _row_spec(tm, cols):
    return pl.BlockSpec((tm, cols), lambda i: (i, 0))


def _layer_norm(z, g, b):
    mu = jnp.mean(z, axis=-1, keepdims=True)
    zc = z - mu
    var = jnp.mean(zc * zc, axis=-1, keepdims=True)
    return zc * lax.rsqrt(var + LN_EPS) * g + b


def _rms_norm(h, g):
    return h * lax.rsqrt(jnp.mean(h * h, axis=-1, keepdims=True) + RMS_EPS) * g


def _dot(a, b):
    return jnp.dot(a, b, preferred_element_type=F32)


def _ffn_ln_kernel(x_ref, win_ref, wout_ref, g_ref, b_ref, o_ref, act_ref, *, d_ff, alpha):
    x = x_ref[...]
    xb = x.astype(BF16)
    for c in range(d_ff // FF_CHUNK):
        lo = c * FF_CHUNK
        gate = _dot(xb, win_ref[:, lo:lo + FF_CHUNK])
        up = _dot(xb, win_ref[:, d_ff + lo:d_ff + lo + FF_CHUNK])
        act_ref[:, lo:lo + FF_CHUNK] = (gate * jax.nn.sigmoid(gate) * up).astype(BF16)
    y = _dot(act_ref[...], wout_ref[...])
    o_ref[...] = _layer_norm(alpha * x + 0.5 * y, g_ref[...], b_ref[...])


def _ffn_ln(x, w_in, w_out, g, b, alpha):
    m, d = x.shape
    d_ff = w_out.shape[0]
    assert d_ff % FF_CHUNK == 0 and m % ROW_TILE == 0
    tm = ROW_TILE
    return pl.pallas_call(
        functools.partial(_ffn_ln_kernel, d_ff=d_ff, alpha=alpha),
        grid=(m // tm,),
        in_specs=[_row_spec(tm, d), _const_spec(w_in.shape), _const_spec(w_out.shape),
                  _const_spec((1, d)), _const_spec((1, d))],
        out_specs=_row_spec(tm, d),
        out_shape=jax.ShapeDtypeStruct((m, d), F32),
        scratch_shapes=[pltpu.VMEM((tm, d_ff), BF16)],
        compiler_params=_params(1),
        name="ffn_ln",
    )(x, w_in, w_out, g, b)


def _post_mix_kernel(a_ref, x_ref, p_ref, wo_ref, win_ref, wout_ref, wg_ref, bg_ref, wp_ref,
                     g_ref, b_ref, o_ref, act_ref, xs_ref, *, d_ff, alpha):
    tm = x_ref.shape[0]
    for r in range(0, tm, SUB_ROWS):
        rows = pl.ds(r, SUB_ROWS)
        y = _dot(a_ref[rows, :], wo_ref[...])
        xs_ref[rows, :] = _layer_norm(alpha * x_ref[rows, :] + y, g_ref[1:2, :], b_ref[1:2, :])
    x1 = xs_ref[...]
    xb = x1.astype(BF16)
    for c in range(d_ff // FF_CHUNK):
        lo = c * FF_CHUNK
        gate = _dot(xb, win_ref[:, lo:lo + FF_CHUNK])
        up = _dot(xb, win_ref[:, d_ff + lo:d_ff + lo + FF_CHUNK])
        act_ref[:, lo:lo + FF_CHUNK] = (gate * jax.nn.sigmoid(gate) * up).astype(BF16)
    y = _dot(act_ref[...], wout_ref[...])
    xs_ref[...] = _layer_norm(alpha * x1 + 0.5 * y, g_ref[2:3, :], b_ref[2:3, :])
    for r in range(0, tm, SUB_ROWS):
        rows = pl.ds(r, SUB_ROWS)
        x2 = xs_ref[rows, :]
        gate = jax.nn.sigmoid(_dot(x2.astype(BF16), wg_ref[...]) + bg_ref[...])
        proj = _dot(p_ref[0, rows, :].astype(BF16), wp_ref[...])
        o_ref[rows, :] = _layer_norm(alpha * x2 + gate * proj, g_ref[3:4, :], b_ref[3:4, :])


def _post_mix(a, x, p_all, layer, wo, w_in, w_out, wg, bg, wp, g4, b4, alpha):
    m, d = x.shape
    k = a.shape[1]
    pd = p_all.shape[-1]
    d_ff = w_out.shape[0]
    assert d_ff % FF_CHUNK == 0 and m % ROW_TILE == 0
    tm = ROW_TILE
    return pl.pallas_call(
        functools.partial(_post_mix_kernel, d_ff=d_ff, alpha=alpha),
        grid=(m // tm,),
        in_specs=[_row_spec(tm, k), _row_spec(tm, d),
                  pl.BlockSpec((1, tm, pd), lambda i: (layer, i, 0)),
                  _const_spec(wo.shape), _const_spec(w_in.shape), _const_spec(w_out.shape),
                  _const_spec(wg.shape), _const_spec((1, d)), _const_spec(wp.shape),
                  _const_spec(g4.shape), _const_spec(b4.shape)],
        out_specs=_row_spec(tm, d),
        out_shape=jax.ShapeDtypeStruct((m, d), F32),
        scratch_shapes=[pltpu.VMEM((tm, d_ff), BF16), pltpu.VMEM((tm, d), F32)],
        compiler_params=_params(1),
        name="post_mix",
    )(a, x, p_all, wo, w_in, w_out, wg, bg, wp, g4, b4)


def _fox_proj_kernel(x_ref, wq_ref, wk_ref, wv_ref, wft_ref, q_ref, k_ref, v_ref, ft_ref, *, scale):
    for r in range(0, x_ref.shape[1], SUB_ROWS):
        rows = pl.ds(r, SUB_ROWS)
        xb = x_ref[0, rows, :].astype(BF16)
        q_ref[0, rows, :] = (_dot(xb, wq_ref[...]) * scale).astype(BF16)
        k_ref[0, rows, :] = _dot(xb, wk_ref[...]).astype(BF16)
        v_ref[0, rows, :] = _dot(xb, wv_ref[...]).astype(BF16)
        ft_ref[0, :, rows] = lax.dot_general(wft_ref[...], xb, _NT, preferred_element_type=F32)


def _fox_proj(x3, wq, wk, wv, wft, scale):
    bsz, s, d = x3.shape
    h = wft.shape[0]
    tm = ROW_TILE
    tok = pl.BlockSpec((1, tm, d), lambda b, i: (b, i, 0))
    return pl.pallas_call(
        functools.partial(_fox_proj_kernel, scale=scale),
        grid=(bsz, s // tm),
        in_specs=[tok, _const_spec(wq.shape), _const_spec(wk.shape), _const_spec(wv.shape),
                  _const_spec(wft.shape)],
        out_specs=[tok, tok, tok, pl.BlockSpec((1, h, tm), lambda b, i: (b, 0, i))],
        out_shape=[jax.ShapeDtypeStruct((bsz, s, d), BF16)] * 3
        + [jax.ShapeDtypeStruct((bsz, h, s), F32)],
        compiler_params=_params(2),
        name="fox_proj",
    )(x3, wq, wk, wv, wft)


def _fox_decay_kernel(f_ref, bf_ref, o_ref):
    f = f_ref[0] + bf_ref[...]
    c = jnp.minimum(f, 0.0) - jnp.log1p(jnp.exp(-jnp.abs(f)))
    lane = lax.broadcasted_iota(jnp.int32, c.shape, 1)
    shift = 1
    while shift < c.shape[1]:
        c = c + jnp.where(lane >= shift, pltpu.roll(c, shift, axis=1), 0.0)
        shift *= 2
    o_ref[0] = -LOG2E * c


def _fox_decay(ft, b_f):
    bsz, h, s = ft.shape
    blk = pl.BlockSpec((1, h, s), lambda b: (b, 0, 0))
    return pl.pallas_call(
        _fox_decay_kernel,
        grid=(bsz,),
        in_specs=[blk, _const_spec((h, 1))],
        out_specs=blk,
        out_shape=jax.ShapeDtypeStruct((bsz, h, s), F32),
        compiler_params=_params(1),
        name="fox_decay",
    )(ft, b_f)


def _kv_shared_kernel(x_ref, wd_ref, g_ref, wk_ref, wv_ref, c1_ref, c2_ref,
                      kn_ref, v_ref, kr_ref, *, rank):
    xb = x_ref[...].astype(BF16)
    h = _dot(xb, wd_ref[...])
    c = _rms_norm(h[:, :rank], g_ref[...]).astype(BF16)
    kn_ref[...] = _dot(c, wk_ref[...]).astype(BF16)
    v_ref[...] = _dot(c, wv_ref[...]).astype(BF16)
    hr = h[:, rank:]
    kr = hr * c1_ref[...] + pltpu.roll(hr, LANES // 2, axis=1) * c2_ref[...]
    kr_ref[...] = kr.astype(BF16)


def _kv_ffn_kernel(x_ref, wd_ref, kvg_ref, wk_ref, wv_ref, c1_ref, c2_ref, win_ref, wout_ref,
                   g_ref, b_ref, kn_ref, v_ref, kr_ref, o_ref, act_ref, *, rank, d_ff, alpha):
    _kv_shared_kernel(x_ref, wd_ref, kvg_ref, wk_ref, wv_ref, c1_ref, c2_ref, kn_ref, v_ref, kr_ref,
                      rank=rank)
    _ffn_ln_kernel(x_ref, win_ref, wout_ref, g_ref, b_ref, o_ref, act_ref, d_ff=d_ff, alpha=alpha)


def _kv_ffn(x, wd, kvg, wk, wv, c1, c2, w_in, w_out, g, b, rank, alpha):
    m, d = x.shape
    s = c1.shape[0]
    d_ff = w_out.shape[0]
    assert d_ff % FF_CHUNK == 0 and m % ROW_TILE == 0
    tm = ROW_TILE
    nblk = s // tm
    tab = pl.BlockSpec((tm, LANES), lambda i: (i % nblk, 0))
    hv = wk.shape[1]
    return pl.pallas_call(
        functools.partial(_kv_ffn_kernel, rank=rank, d_ff=d_ff, alpha=alpha),
        grid=(m // tm,),
        in_specs=[_row_spec(tm, d), _const_spec(wd.shape), _const_spec((1, rank)),
                  _const_spec(wk.shape), _const_spec(wv.shape), tab, tab,
                  _const_spec(w_in.shape), _const_spec(w_out.shape),
                  _const_spec((1, d)), _const_spec((1, d))],
        out_specs=[_row_spec(tm, hv), _row_spec(tm, hv), _row_spec(tm, LANES), _row_spec(tm, d)],
        out_shape=[jax.ShapeDtypeStruct((m, hv), BF16), jax.ShapeDtypeStruct((m, hv), BF16),
                   jax.ShapeDtypeStruct((m, LANES), BF16), jax.ShapeDtypeStruct((m, d), F32)],
        scratch_shapes=[pltpu.VMEM((tm, d_ff), BF16)],
        compiler_params=_params(1),
        name="kv_ffn",
    )(x, wd, kvg, wk, wv, c1, c2, w_in, w_out, g, b)


def _mla_q_kernel(x_ref, wdq_ref, g_ref, wn_ref, wr_ref, c1_ref, c2_ref, q_ref, *, heads, scale):
    for r in range(0, x_ref.shape[0], SUB_ROWS):
        rows = pl.ds(r, SUB_ROWS)
        xb = x_ref[rows, :].astype(BF16)
        cq = _rms_norm(_dot(xb, wdq_ref[...]), g_ref[...]).astype(BF16)
        qn = _dot(cq, wn_ref[...])
        qr = _dot(cq, wr_ref[...])
        c1 = c1_ref[rows, :]
        c2 = c2_ref[rows, :]
        for h in range(heads):
            lo = h * LANES
            q_ref[rows, 2 * lo:2 * lo + LANES] = (qn[:, lo:lo + LANES] * scale).astype(BF16)
            hr = qr[:, lo:lo + LANES]
            roped = hr * c1 + pltpu.roll(hr, LANES // 2, axis=1) * c2
            q_ref[rows, 2 * lo + LANES:2 * lo + 2 * LANES] = (roped * scale).astype(BF16)


def _mla_q(x, wdq, g, wn, wr, c1, c2, heads, scale):
    m, d = x.shape
    s = c1.shape[0]
    tm = ROW_TILE
    nblk = s // tm
    tab = pl.BlockSpec((tm, LANES), lambda i: (i % nblk, 0))
    qw = heads * 2 * LANES
    return pl.pallas_call(
        functools.partial(_mla_q_kernel, heads=heads, scale=scale),
        grid=(m // tm,),
        in_specs=[_row_spec(tm, d), _const_spec(wdq.shape), _const_spec((1, wdq.shape[1])),
                  _const_spec(wn.shape), _const_spec(wr.shape), tab, tab],
        out_specs=_row_spec(tm, qw),
        out_shape=jax.ShapeDtypeStruct((m, qw), BF16),
        compiler_params=_params(1),
        name="mla_q",
    )(x, wdq, g, wn, wr, c1, c2)


def _logits_tile(s_ref, q, k, diag, bias=None):
    t = s_ref.shape[0]
    for n0 in range(0, t, MXU_TILE):
        r0 = n0 if diag else 0
        s = lax.dot_general(q[r0:], k[n0:n0 + MXU_TILE], _NT, preferred_element_type=F32)
        if bias is not None:
            s = s + bias[:, n0:n0 + MXU_TILE]
        s_ref[r0:, n0:n0 + MXU_TILE] = s


def _softmax_rows(s_ref, p_ref, m_ref, l_ref, a_ref, idx, first, diag):
    t, tk = s_ref.shape
    for r in range(0, t, SOFTMAX_ROWS):
        rows = pl.ds(r, SOFTMAX_ROWS)
        live = min(tk, -(-(r + SOFTMAX_ROWS) // LANES) * LANES) if diag else tk
        s = s_ref[rows, :live]
        if diag:
            row = r + lax.broadcasted_iota(jnp.int32, s.shape, 0)
            col = lax.broadcasted_iota(jnp.int32, s.shape, 1)
            s = jnp.where(col <= row, s, -jnp.inf)
        m_cur = jnp.max(s, axis=1, keepdims=True)
        if first:
            m_next = jnp.broadcast_to(m_cur, (SOFTMAX_ROWS, LANES))
        else:
            m_prev = m_ref[idx, rows, :]
            m_next = jnp.maximum(m_prev, m_cur)
        p = jnp.exp2(s - jnp.concatenate([m_next] * (live // LANES), axis=1))
        l_cur = jnp.sum(p, axis=1, keepdims=True)
        if first:
            l_ref[idx, rows, :] = jnp.broadcast_to(l_cur, (SOFTMAX_ROWS, LANES))
        else:
            alpha = jnp.exp2(m_prev - m_next)
            l_ref[idx, rows, :] = alpha * l_ref[idx, rows, :] + l_cur
            a_ref[rows, :] = alpha
        m_ref[idx, rows, :] = m_next
        p_ref[rows, :live] = p.astype(BF16)
        read_end = (r // MXU_TILE + 1) * MXU_TILE if diag else tk
        if live < read_end:
            p_ref[rows, live:read_end] = jnp.zeros((SOFTMAX_ROWS, read_end - live), BF16)


def _values_tile(acc_ref, idx, a_ref, p_ref, v, first, diag):
    t = p_ref.shape[0]
    if not diag:
        pv = _dot(p_ref[...], v)
        acc_ref[idx] = pv if first else a_ref[...] * acc_ref[idx] + pv
        return
    blocks = [None] * (t // MXU_TILE)
    if not first:
        scaled = a_ref[...] * acc_ref[idx]
        blocks = [scaled[r:r + MXU_TILE] for r in range(0, t, MXU_TILE)]
    for n0 in range(0, t, MXU_TILE):
        pv = _dot(p_ref[n0:, n0:n0 + MXU_TILE], v[n0:n0 + MXU_TILE])
        for r in range(n0, t, MXU_TILE):
            part = pv[r - n0:r - n0 + MXU_TILE]
            b = r // MXU_TILE
            blocks[b] = part if blocks[b] is None else blocks[b] + part
    acc_ref[idx] = jnp.concatenate(blocks, axis=0)


def _causal_pipeline(n_tiles, n_heads, logits, softmax, values):
    last = n_tiles - 1
    for k in range(n_tiles + 2):
        for h in range(n_heads):
            if 0 <= k - 2:
                values(h, k - 2, k % 2, k - 2 == 0, k - 2 == last)
            if k < n_tiles:
                logits(h, k, k % 2, k == 0, k == last)
            if 0 <= k - 1 < n_tiles:
                softmax(h, k - 1, (k - 1) % 2, k - 1 == 0, k - 1 == last)


def _attn_scratch(t, n_heads):
    return ([pltpu.VMEM((2 * n_heads, t, t), F32), pltpu.VMEM((2 * n_heads, t, t), BF16),
             pltpu.VMEM((2 * n_heads, t, LANES), F32)] + [pltpu.VMEM((n_heads, t, LANES), F32)] * 3)


def _fox_attn_kernel(q_ref, k_ref, v_ref, nc_ref, o_ref, s_ref, p_ref, a_ref, m_ref, l_ref, acc_ref,
                     *, t, hd, n_q):
    qi = pl.program_id(2)
    lane = lax.broadcasted_iota(jnp.int32, (t, LANES), 1)

    def logits(h, j, slot, first, diag):
        q = q_ref[0]
        q_h = jnp.where((lane < hd) if h == 0 else (lane >= hd), q, jnp.zeros_like(q))
        _logits_tile(s_ref.at[2 * h + slot], q_h, k_ref[0, j * t:(j + 1) * t, :], diag,
                     bias=nc_ref[0, 0, h:h + 1, j * t:(j + 1) * t])

    def softmax(h, j, slot, first, diag):
        i = 2 * h + slot
        _softmax_rows(s_ref.at[i], p_ref.at[i], m_ref, l_ref, a_ref.at[i], h, first, diag)

    def values(h, j, slot, first, diag):
        i = 2 * h + slot
        _values_tile(acc_ref, h, a_ref.at[i], p_ref.at[i], v_ref[0, j * t:(j + 1) * t, :], first, diag)

    for c in range(n_q):
        @pl.when(qi == c)
        def _(c=c):
            _causal_pipeline(c + 1, 2, logits, softmax, values)
            o = jnp.where(lane < hd, acc_ref[0] / l_ref[0], acc_ref[1] / l_ref[1])
            o_ref[0] = o.astype(o_ref.dtype)


def _fox_attn(q, k, v, negcum, hd):
    bsz, s, d = q.shape
    t = ATTN_TILE
    assert 2 * hd == LANES and s % t == 0
    pairs = d // LANES
    nblk = s // t
    qspec = pl.BlockSpec((1, t, LANES), lambda b, p, i: (b, i, p))
    kvspec = pl.BlockSpec((1, s, LANES), lambda b, p, i: (b, 0, p))
    return pl.pallas_call(
        functools.partial(_fox_attn_kernel, t=t, hd=hd, n_q=nblk),
        grid=(bsz, pairs, nblk),
        in_specs=[qspec, kvspec, kvspec,
                  pl.BlockSpec((1, 1, 2, s), lambda b, p, i: (b, p, 0, 0))],
        out_specs=qspec,
        out_shape=jax.ShapeDtypeStruct((bsz, s, d), BF16),
        scratch_shapes=_attn_scratch(t, 2),
        compiler_params=_params(3),
        name="fox_attn",
    )(q, k, v, negcum.reshape(bsz, pairs, 2, s))


def _mla_attn_kernel(q_ref, kn_ref, kr_ref, v_ref, o_ref, kf_ref, s_ref, p_ref, a_ref,
                     m_ref, l_ref, acc_ref, *, t, n_q):
    qi = pl.program_id(2)

    @pl.when(qi == 0)
    def _():
        kf_ref[:, :LANES] = kn_ref[0]
        kf_ref[:, LANES:] = kr_ref[0]

    def logits(h, j, slot, first, diag):
        _logits_tile(s_ref.at[slot], q_ref[0], kf_ref[j * t:(j + 1) * t, :], diag)

    def softmax(h, j, slot, first, diag):
        _softmax_rows(s_ref.at[slot], p_ref.at[slot], m_ref, l_ref, a_ref.at[slot], 0, first, diag)

    def values(h, j, slot, first, diag):
        _values_tile(acc_ref, 0, a_ref.at[slot], p_ref.at[slot], v_ref[0, j * t:(j + 1) * t, :],
                     first, diag)

    for c in range(n_q):
        @pl.when(qi == c)
        def _(c=c):
            _causal_pipeline(c + 1, 1, logits, softmax, values)
            o_ref[0] = (acc_ref[0] / l_ref[0]).astype(o_ref.dtype)


def _mla_attn(q, kn, kr, v, heads):
    bsz, s, _ = q.shape
    t = ATTN_TILE
    assert s % t == 0
    nblk = s // t
    return pl.pallas_call(
        functools.partial(_mla_attn_kernel, t=t, n_q=nblk),
        grid=(bsz, heads, nblk),
        in_specs=[pl.BlockSpec((1, t, 2 * LANES), lambda b, h, i: (b, i, h)),
                  pl.BlockSpec((1, s, LANES), lambda b, h, i: (b, 0, h)),
                  pl.BlockSpec((1, s, LANES), lambda b, h, i: (b, 0, 0)),
                  pl.BlockSpec((1, s, LANES), lambda b, h, i: (b, 0, h))],
        out_specs=pl.BlockSpec((1, t, LANES), lambda b, h, i: (b, i, h)),
        out_shape=jax.ShapeDtypeStruct((bsz, s, heads * LANES), BF16),
        scratch_shapes=[pltpu.VMEM((s, 2 * LANES), BF16)] + _attn_scratch(t, 1),
        compiler_params=_params(3),
        name="mla_attn",
    )(q, kn, kr, v)


def _rot_half_cols(w):
    half = w.shape[-1] // 2
    return jnp.concatenate([-w[..., half:], w[..., :half]], axis=-1)


def kernel(x, p, ffn1_w_in, ffn1_w_out, ffn2_w_in, ffn2_w_out, ln_g, ln_b, ple_w_gate, ple_b_gate, ple_w_proj, fox_w_in, fox_b_f, fox_w_o, mla_w_dq, mla_q_norm, mla_w_uq, mla_w_o, kv_w_down, kv_norm, kv_w_up):
    bsz, s, d = x.shape
    depth = ffn1_w_in.shape[0]
    n_a = fox_w_in.shape[0]
    m = bsz * s
    alpha = (2 * depth) ** 0.25
    fox_heads = fox_b_f.shape[1]
    fox_hd = d // fox_heads
    rank, mla_heads = kv_w_up.shape[0], kv_w_up.shape[1]
    rope_dim = kv_w_down.shape[1] - rank
    nope = mla_w_uq.shape[3] - rope_dim
    assert nope == LANES and kv_w_up.shape[2] == 2 * LANES and 2 * rope_dim == LANES

    half = rope_dim // 2
    inv = ROPE_THETA ** (-jnp.arange(half, dtype=F32) * (2.0 / rope_dim))
    ang = jnp.arange(s, dtype=F32)[:, None] * inv[None, :]
    pad = jnp.zeros((s, LANES - rope_dim), F32)
    c1 = jnp.concatenate([jnp.cos(ang), jnp.cos(ang), pad], axis=1)
    c2 = jnp.concatenate([jnp.sin(ang), jnp.sin(ang), pad], axis=1)

    p2 = p.reshape(depth, m, p.shape[-1])
    row = lambda a: a.reshape(1, -1)
    xs = x.reshape(m, d)
    kn = kr = vs = None
    for i in range(depth):
        if i == n_a:
            wdr = kv_w_down[:, rank:]
            wd = jnp.concatenate([kv_w_down[:, :rank], wdr, _rot_half_cols(wdr)], axis=1).astype(BF16)
            wk = kv_w_up[:, :, :nope].reshape(rank, -1).astype(BF16)
            wv = kv_w_up[:, :, nope:].reshape(rank, -1).astype(BF16)
            kn, vs, kr, xs = _kv_ffn(xs, wd, row(kv_norm), wk, wv, c1, c2,
                                     ffn1_w_in[i].astype(BF16), ffn1_w_out[i].astype(BF16),
                                     row(ln_g[i, 0]), row(ln_b[i, 0]), rank, alpha)
            kn, vs, kr = (a.reshape(bsz, s, -1) for a in (kn, vs, kr))
        else:
            xs = _ffn_ln(xs, ffn1_w_in[i].astype(BF16), ffn1_w_out[i].astype(BF16),
                         row(ln_g[i, 0]), row(ln_b[i, 0]), alpha)
        if i < n_a:
            w = fox_w_in[i]
            q, k, v, ft = _fox_proj(xs.reshape(bsz, s, d), w[:, :d].astype(BF16),
                                    w[:, d:2 * d].astype(BF16), w[:, 2 * d:3 * d].astype(BF16),
                                    w[:, 3 * d:].T.astype(BF16), LOG2E * fox_hd ** -0.5)
            negcum = _fox_decay(ft, fox_b_f[i].reshape(-1, 1))
            o = _fox_attn(q, k, v, negcum, fox_hd)
            w_o = fox_w_o[i]
        else:
            j = i - n_a
            wuq = mla_w_uq[j]
            q_rank = wuq.shape[0]
            wn = wuq[:, :, :nope].reshape(q_rank, -1).astype(BF16)
            wr3 = wuq[:, :, nope:]
            wr = jnp.concatenate([wr3, _rot_half_cols(wr3)], axis=-1).reshape(q_rank, -1).astype(BF16)
            q = _mla_q(xs, mla_w_dq[j].astype(BF16), row(mla_q_norm[j]), wn, wr, c1, c2,
                       mla_heads, LOG2E * (nope + rope_dim) ** -0.5)
            o = _mla_attn(q.reshape(bsz, s, -1), kn, kr, vs, mla_heads)
            w_o = mla_w_o[j]
        xs = _post_mix(o.reshape(m, -1), xs, p2, i, w_o.astype(BF16),
                       ffn2_w_in[i].astype(BF16), ffn2_w_out[i].astype(BF16),
                       ple_w_gate[i].astype(BF16), row(ple_b_gate[i]), ple_w_proj[i].astype(BF16),
                       ln_g[i], ln_b[i], alpha)
    return xs.reshape(bsz, s, d)
```
